```python
import math
import jax, jax.numpy as jnp
from jax import lax
import numpy as np

D_MODEL = 1024
BATCH = 8
SEQ = 2048
DEPTH = 2
DEC_BATCH = 32
DEC_SEQ = 4
PAST_LEN = 16384
PAGE_SIZE = 128

N_MIXERS = 2
N_ATTN_LAYERS = (DEPTH + 1) // 2
N_RWKV_LAYERS = DEPTH // 2
ATTN_HEAD_DIM = 64
ATTN_HEADS = D_MODEL // (2 * ATTN_HEAD_DIM)
Q_BLOCK = 128
SUBLN_EPS = 1e-5
RWKV_HEAD = 64
RWKV_HEADS = D_MODEL // RWKV_HEAD
DECAY_LORA = 64
ICLR_LORA = 64
GATE_LORA = 160
LNX_EPS = 64e-5
D_FF = 2816
CONV_W = 3
RMS_EPS = 1e-6

kernel_name = 'diffattn_rwkv7_convffn_hybrid_step'


def rmsnorm(x, g, eps=RMS_EPS):
    xf = x.astype(jnp.float32)
    y = xf * lax.rsqrt(jnp.mean(xf * xf, axis=-1, keepdims=True) + eps)
    return (y * g.astype(jnp.float32)).astype(x.dtype)


def alibi_slopes():
    return jnp.asarray(2.0 ** (-8.0 * np.arange(1, ATTN_HEADS + 1) / ATTN_HEADS), jnp.float32)


def attn_qkv(h, w_qkv):
    B, T, _ = h.shape
    q, k, v = jnp.split(h @ w_qkv, 3, axis=-1)
    return (q.reshape(B, T, ATTN_HEADS, 2, ATTN_HEAD_DIM),
            k.reshape(B, T, ATTN_HEADS, 2 * ATTN_HEAD_DIM),
            v.reshape(B, T, ATTN_HEADS, 2 * ATTN_HEAD_DIM))


def diff_attend(q, q_pos, segments, lam):
    f32 = jnp.float32
    qf = q.astype(f32) * ATTN_HEAD_DIM ** -0.5
    slopes = alibi_slopes()[:, None, None, None]
    scores = []
    for k, v, k_pos in segments:
        kf = k.astype(f32).reshape(k.shape[0], k.shape[1], ATTN_HEADS, 2, ATTN_HEAD_DIM)
        s = jnp.einsum('bqhcd,bkhcd->bhcqk', qf, kf)
        dist = (q_pos[:, None] - k_pos[None, :]).astype(f32)
        scores.append(jnp.where(dist >= 0, s - slopes * dist, -jnp.inf))
    p = jax.nn.softmax(jnp.concatenate(scores, axis=-1), axis=-1)
    a = p[:, :, 0] - lam * p[:, :, 1]
    outs = []
    start = 0
    for k, v, k_pos in segments:
        n = k.shape[1]
        outs.append(jnp.einsum('bhqk,bkhe->bqhe', a[..., start:start + n], v.astype(f32)))
        start += n
    return sum(outs)


def blocked_self_attend(q, k, v, lam):
    B, T = q.shape[0], q.shape[1]
    pos = jnp.arange(T)
    nb = T // Q_BLOCK
    qb = jnp.moveaxis(q.reshape(B, nb, Q_BLOCK, ATTN_HEADS, 2, ATTN_HEAD_DIM), 1, 0)
    pb = pos.reshape(nb, Q_BLOCK)
    ob = lax.map(lambda a: diff_attend(a[0], a[1], ((k, v, pos),), lam), (qb, pb))
    return jnp.moveaxis(ob, 0, 1).reshape(B, T, ATTN_HEADS, 2 * ATTN_HEAD_DIM)


def attn_out(o, lam_init, subln_g, w_o, dtype):
    B, T = o.shape[0], o.shape[1]
    o = o * lax.rsqrt(jnp.mean(o * o, axis=-1, keepdims=True) + SUBLN_EPS) * subln_g.astype(jnp.float32) * (1.0 - lam_init)
    return o.reshape(B, T, D_MODEL).astype(dtype) @ w_o


def wkv7_scan(S0, r, w, k, v, a, b):
    def step(S, inp):
        r_t, w_t, k_t, v_t, a_t, b_t = inp
        sa = jnp.einsum('bhij,bhj->bhi', S, a_t)
        S = S * w_t[:, :, None, :] + sa[..., None] * b_t[:, :, None, :] + v_t[..., None] * k_t[:, :, None, :]
        return S, jnp.einsum('bhij,bhj->bhi', S, r_t)
    xs = tuple(jnp.moveaxis(t, 1, 0) for t in (r, w, k, v, a, b))
    S, ys = lax.scan(step, S0, xs)
    return jnp.moveaxis(ys, 0, 1), S


def rwkv7_mixer(h, shift0, wkv0, mix, w_rkv, w_decay0, w_decay1, w_decay2, w_iclr0, w_iclr1, w_iclr2,
                g1, g2, k_k, k_a, r_k, lnx_g, lnx_b, w_o):
    f32 = jnp.float32
    B, T, D = h.shape
    prev = jnp.concatenate([shift0[:, None, :].astype(h.dtype), h[:, :-1]], axis=1)
    xx = prev - h
    xm = h[None] + xx[None] * mix[:, None, None, :]
    rkv = jnp.einsum('sbtd,sde->sbte', xm[:3], w_rkv)
    r, k, v = rkv[0], rkv[1], rkv[2]
    xw, xa, xg = xm[3], xm[4], xm[5]
    w_log = -jax.nn.softplus(-(w_decay0 + jnp.tanh(xw @ w_decay1) @ w_decay2)) - 0.5
    decay = jnp.exp(-jnp.exp(w_log.astype(f32)))
    a = jax.nn.sigmoid(w_iclr0 + (xa @ w_iclr1) @ w_iclr2)
    g = jax.nn.sigmoid(xg @ g1) @ g2
    heads = lambda t: t.astype(f32).reshape(B, T, RWKV_HEADS, RWKV_HEAD)
    kk = heads(k * k_k)
    kk = kk / jnp.maximum(jnp.sqrt(jnp.sum(kk * kk, axis=-1, keepdims=True)), 1e-12)
    k = k * (1 + (a - 1) * k_a)
    rh, kh, vh, ah = heads(r), heads(k), heads(v), heads(a)
    y, S = wkv7_scan(wkv0.astype(f32), rh, heads(decay), kh, vh, -kk, kk * ah)
    mu = jnp.mean(y, axis=-1, keepdims=True)
    var = jnp.mean(jnp.square(y - mu), axis=-1, keepdims=True)
    y = (y - mu) * lax.rsqrt(var + LNX_EPS) * lnx_g.astype(f32).reshape(RWKV_HEADS, RWKV_HEAD) \
        + lnx_b.astype(f32).reshape(RWKV_HEADS, RWKV_HEAD)
    y = y + jnp.sum(rh * kh * r_k.astype(f32), axis=-1, keepdims=True) * vh
    out = (y.reshape(B, T, D).astype(h.dtype) * g) @ w_o
    return out, h[:, -1], S


def conv_ffn(h, buf0, w_gate, w_up, conv_w, conv_b, w_down):
    T = h.shape[1]
    u = h @ w_gate
    full = jnp.concatenate([buf0.astype(u.dtype), u], axis=1)
    c = conv_b + sum(full[:, j:j + T] * conv_w[j] for j in range(CONV_W))
    out = (jax.nn.silu(c) * (h @ w_up)) @ w_down
    return out, full[:, -(CONV_W - 1):]


def setup_inputs(seed: int = 0) -> dict:
    key = jax.random.key(seed)
    ks = iter(jax.random.split(key, 64))
    nrm = lambda shape, s: jax.random.normal(next(ks), shape, jnp.float32) * s
    gain = lambda shape: 1.0 + nrm(shape, 0.02)
    n_pages = PAST_LEN // PAGE_SIZE
    n_pool = (5 * DEC_BATCH * n_pages) // 4
    na, nr = N_ATTN_LAYERS, N_RWKV_LAYERS
    H, N, d2 = RWKV_HEADS, RWKV_HEAD, 2 * ATTN_HEAD_DIM
    D, F = D_MODEL, D_FF
    return {
        'x_prompt': nrm((BATCH, SEQ, D), 1.0),
        'x_sample': nrm((DEC_BATCH, DEC_SEQ, D), 1.0),
        'cache_k': nrm((na, n_pool, PAGE_SIZE, ATTN_HEADS, d2), 1.0),
        'cache_v': nrm((na, n_pool, PAGE_SIZE, ATTN_HEADS, d2), 1.0),
        'page_table': jax.random.permutation(next(ks), n_pool)[:DEC_BATCH * n_pages].reshape(DEC_BATCH, n_pages).astype(jnp.int32),
        'state_shift': nrm((nr, DEC_BATCH, D), 1.0),
        'state_wkv': nrm((nr, DEC_BATCH, H, N, N), 0.3),
        'state_ffn_conv': nrm((DEPTH, DEC_BATCH, CONV_W - 1, F), 1.0),
        'attn_norm_g': gain((na, D)),
        'w_qkv': nrm((na, D, 3 * D), D ** -0.5),
        'lambda_q1': nrm((na, ATTN_HEAD_DIM), 0.1),
        'lambda_k1': nrm((na, ATTN_HEAD_DIM), 0.1),
        'lambda_q2': nrm((na, ATTN_HEAD_DIM), 0.1),
        'lambda_k2': nrm((na, ATTN_HEAD_DIM), 0.1),
        'subln_g': gain((na, d2)),
        'w_o_attn': nrm((na, D, D), D ** -0.5),
        'rwkv_norm_g': gain((nr, D)),
        'rwkv_mix': jax.random.uniform(next(ks), (nr, 6, D), jnp.float32),
        'w_rkv': nrm((nr, 3, D, D), D ** -0.5),
        'w_decay0': jax.random.uniform(next(ks), (nr, D), jnp.float32, minval=-6.0, maxval=-1.0),
        'w_decay1': nrm((nr, D, DECAY_LORA), D ** -0.5),
        'w_decay2': nrm((nr, DECAY_LORA, D), 0.1 * DECAY_LORA ** -0.5),
        'w_iclr0': nrm((nr, D), 0.1),
        'w_iclr1': nrm((nr, D, ICLR_LORA), D ** -0.5),
        'w_iclr2': nrm((nr, ICLR_LORA, D), 0.1 * ICLR_LORA ** -0.5),
        'rwkv_g1': nrm((nr, D, GATE_LORA), D ** -0.5),
        'rwkv_g2': nrm((nr, GATE_LORA, D), GATE_LORA ** -0.5),
        'k_k': 0.85 + nrm((nr, D), 0.05),
        'k_a': 1.0 + nrm((nr, D), 0.05),
        'r_k': nrm((nr, H, N), 0.1),
        'lnx_g': gain((nr, D)),
        'lnx_b': nrm((nr, D), 0.01),
        'w_o_rwkv': nrm((nr, D, D), D ** -0.5),
        'ffn_norm_g': gain((DEPTH, D)),
        'ffn_w_gate': nrm((DEPTH, D, F), D ** -0.5),
        'ffn_w_up': nrm((DEPTH, D, F), D ** -0.5),
        'ffn_conv_w': nrm((DEPTH, CONV_W, F), CONV_W ** -0.5),
        'ffn_conv_b': nrm((DEPTH, F), 0.01),
        'ffn_w_down': nrm((DEPTH, F, D), F ** -0.5),
        'final_norm_g': gain((D,)),
    }


def reference(x_prompt, x_sample, cache_k, cache_v, page_table, state_shift, state_wkv, state_ffn_conv,
              attn_norm_g, w_qkv, lambda_q1, lambda_k1, lambda_q2, lambda_k2, subln_g, w_o_attn,
              rwkv_norm_g, rwkv_mix, w_rkv, w_decay0, w_decay1, w_decay2, w_iclr0, w_iclr1, w_iclr2,
              rwkv_g1, rwkv_g2, k_k, k_a, r_k, lnx_g, lnx_b, w_o_rwkv,
              ffn_norm_g, ffn_w_gate, ffn_w_up, ffn_conv_w, ffn_conv_b, ffn_w_down, final_norm_g):
    f32 = jnp.float32
    B = x_prompt.shape[0]
    DB, TS = x_sample.shape[0], x_sample.shape[1]
    past_len = page_table.shape[1] * cache_k.shape[2]
    pos_past = jnp.arange(past_len)
    pos_new = past_len + jnp.arange(TS)
    xp, xs = x_prompt, x_sample
    kp_l, vp_l, ks_l, vs_l = [], [], [], []
    shp_l, shs_l, wkp_l, wks_l = [], [], [], []
    cvp_l, cvs_l = [], []
    for i in range(DEPTH):
        j = i // N_MIXERS
        if i % N_MIXERS == 0:
            lam_init = 0.8 - 0.6 * math.exp(-0.3 * i)
            lam = (jnp.exp(jnp.sum(lambda_q1[j].astype(f32) * lambda_k1[j].astype(f32)))
                   - jnp.exp(jnp.sum(lambda_q2[j].astype(f32) * lambda_k2[j].astype(f32))) + lam_init)
            hp = rmsnorm(xp, attn_norm_g[j])
            q, k, v = attn_qkv(hp, w_qkv[j])
            o = blocked_self_attend(q, k, v, lam)
            xp = xp + attn_out(o, lam_init, subln_g[j], w_o_attn[j], xp.dtype)
            hs = rmsnorm(xs, attn_norm_g[j])
            qs, k_new, v_new = attn_qkv(hs, w_qkv[j])
            k_past = cache_k[j, page_table].reshape(DB, past_len, ATTN_HEADS, 2 * ATTN_HEAD_DIM)
            v_past = cache_v[j, page_table].reshape(DB, past_len, ATTN_HEADS, 2 * ATTN_HEAD_DIM)
            o = diff_attend(qs, pos_new, ((k_past, v_past, pos_past), (k_new, v_new, pos_new)), lam)
            xs = xs + attn_out(o, lam_init, subln_g[j], w_o_attn[j], xs.dtype)
            kp_l.append(k); vp_l.append(v); ks_l.append(k_new); vs_l.append(v_new)
        else:
            params = (rwkv_mix[j], w_rkv[j], w_decay0[j], w_decay1[j], w_decay2[j], w_iclr0[j], w_iclr1[j],
                      w_iclr2[j], rwkv_g1[j], rwkv_g2[j], k_k[j], k_a[j], r_k[j], lnx_g[j], lnx_b[j], w_o_rwkv[j])
            hp = rmsnorm(xp, rwkv_norm_g[j])
            out, sh, S = rwkv7_mixer(hp, jnp.zeros((B, D_MODEL), hp.dtype),
                                     jnp.zeros((B, RWKV_HEADS, RWKV_HEAD, RWKV_HEAD), f32), *params)
            xp = xp + out
            shp_l.append(sh.astype(state_shift.dtype)); wkp_l.append(S.astype(state_wkv.dtype))
            hs = rmsnorm(xs, rwkv_norm_g[j])
            out, sh, S = rwkv7_mixer(hs, state_shift[j], state_wkv[j], *params)
            xs = xs + out
            shs_l.append(sh.astype(state_shift.dtype)); wks_l.append(S.astype(state_wkv.dtype))
        ffn = (ffn_w_gate[i], ffn_w_up[i], ffn_conv_w[i], ffn_conv_b[i], ffn_w_down[i])
        hp = rmsnorm(xp, ffn_norm_g[i])
        out, buf = conv_ffn(hp, jnp.zeros((B, CONV_W - 1, D_FF), hp.dtype), *ffn)
        xp = xp + out
        cvp_l.append(buf.astype(state_ffn_conv.dtype))
        hs = rmsnorm(xs, ffn_norm_g[i])
        out, buf = conv_ffn(hs, state_ffn_conv[i], *ffn)
        xs = xs + out
        cvs_l.append(buf.astype(state_ffn_conv.dtype))
    y_prompt = rmsnorm(xp, final_norm_g)
    y_sample = rmsnorm(xs, final_norm_g)
    return (y_prompt, y_sample,
            jnp.stack(kp_l), jnp.stack(vp_l), jnp.stack(ks_l), jnp.stack(vs_l),
            jnp.stack(shp_l), jnp.stack(shs_l), jnp.stack(wkp_l), jnp.stack(wks_l),
            jnp.stack(cvp_l), jnp.stack(cvs_l))
```

```python
import functools
import math

import jax
import jax.numpy as jnp
from jax import lax
from jax.experimental import pallas as pl
from jax.experimental.pallas import tpu as pltpu

F32 = jnp.float32
BF16 = jnp.bfloat16

RMS_EPS = 1e-6
SUBLN_EPS = 1e-5
LNX_EPS = 64e-5
ATTN_HEAD_DIM = 64
RWKV_HEAD = 64
LANES = 128
NEG_BIG = -1e30
VMEM_LIMIT_BYTES = 56 * 1024 * 1024

ROW_TILE = 512
FLASH_TILE = 256
FFN_CHUNK = 256
WKV_CHUNK = 64

NT_DIMS = (((1,), (1,)), ((), ()))
TN_DIMS = (((0,), (0,)), ((), ()))


def _params(n_axes):
    return pltpu.CompilerParams(dimension_semantics=("arbitrary",) * n_axes,
                                vmem_limit_bytes=VMEM_LIMIT_BYTES)


def _dot(a, b):
    return jnp.dot(a, b, preferred_element_type=F32)


def _dot_nt(a, b):
    return lax.dot_general(a, b, NT_DIMS, preferred_element_type=F32)


def _dot_tn(a, b):
    return lax.dot_general(a, b, TN_DIMS, preferred_element_type=F32)


def _rms(x, g, eps=RMS_EPS):
    return x * lax.rsqrt(jnp.mean(x * x, axis=-1, keepdims=True) + eps) * g


def _const_spec(shape):
    nd = len(shape)
    return pl.BlockSpec(shape, lambda *_: (0,) * nd)


def _row_spec(tm, d):
    return pl.BlockSpec((tm, d), lambda i: (i, 0))


def _seg_sum(x, pd, pu):
    hi = x.astype(BF16)
    lo = (x - hi.astype(F32)).astype(BF16)
    s = _dot(hi, pd) + _dot(lo, pd)
    s1 = s.astype(BF16)
    r = s - s1.astype(F32)
    s2 = r.astype(BF16)
    s3 = (r - s2.astype(F32)).astype(BF16)
    return _dot(s1, pu) + _dot(s2, pu) + _dot(s3, pu)


def _lambda(lq1, lk1, lq2, lk2, lam_init):
    s1 = jnp.sum(lq1[...] * lk1[...], axis=-1, keepdims=True)
    s2 = jnp.sum(lq2[...] * lk2[...], axis=-1, keepdims=True)
    return jnp.exp(s1) - jnp.exp(s2) + lam_init


def _qkv_body(x_ref, g_ref, w_ref, qb_ref, kb_ref, vb_ref, k_ref, v_ref):
    d = x_ref.shape[1]
    h = _rms(x_ref[...], g_ref[...]).astype(BF16)
    q = _dot(h, w_ref[:, 0:d])
    qb_ref[...] = (q * (ATTN_HEAD_DIM ** -0.5)).astype(BF16)
    k = _dot(h, w_ref[:, d:2 * d])
    k_ref[...] = k
    kb_ref[...] = k.astype(BF16)
    v = _dot(h, w_ref[:, 2 * d:3 * d])
    v_ref[...] = v
    vb_ref[...] = v.astype(BF16)


def _qkv(x, g, w_bf, tm):
    m, d = x.shape
    bf = jax.ShapeDtypeStruct((m, d), BF16)
    f = jax.ShapeDtypeStruct((m, d), F32)
    return pl.pallas_call(
        _qkv_body,
        grid=(m // tm,),
        in_specs=[_row_spec(tm, d), _const_spec((1, d)), _const_spec((d, 3 * d))],
        out_specs=[_row_spec(tm, d)] * 5,
        out_shape=[bf, bf, bf, f, f],
        compiler_params=_params(1),
        name="qkv_proj",
    )(x, g.reshape(1, d), w_bf)


def _flash_body(q_ref, k_ref, v_ref, lq1, lk1, lq2, lk2, sg_ref, o_ref, *, tile, n_heads, lam_init):
    qi = pl.program_id(1)
    q0 = qi * tile
    n_kv = qi + 1
    lam = _lambda(lq1, lk1, lq2, lk2, lam_init)
    row = lax.broadcasted_iota(jnp.int32, (tile, tile), 0)
    col = lax.broadcasted_iota(jnp.int32, (tile, tile), 1)
    col_row = lax.broadcasted_iota(jnp.int32, (1, tile), 1)
    lane = lax.broadcasted_iota(jnp.int32, (tile, 2 * ATTN_HEAD_DIM), 1)
    sg = sg_ref[...]
    for h in range(n_heads):
        slope = 2.0 ** (-8.0 * (h + 1) / n_heads)
        hs = slice(h * 2 * ATTN_HEAD_DIM, (h + 1) * 2 * ATTN_HEAD_DIM)
        qh = q_ref[:, hs]
        qmaps = (jnp.where(lane < ATTN_HEAD_DIM, qh, jnp.zeros_like(qh)),
                 jnp.where(lane >= ATTN_HEAD_DIM, qh, jnp.zeros_like(qh)))

        def kv_step(j, carry, hs=hs, qmaps=qmaps, slope=slope):
            k0 = pl.multiple_of(j * tile, tile)
            kj = k_ref[pl.ds(k0, tile), hs]
            vj = v_ref[pl.ds(k0, tile), hs]
            bias = slope * (k0 - q0 + col_row).astype(F32)
            keep = (k0 + col) <= (q0 + row)
            new = []
            for c in range(2):
                m_prev, l_prev, acc = carry[c]
                s = jnp.where(keep, _dot_nt(qmaps[c], kj) + bias, NEG_BIG)
                m_new = jnp.maximum(m_prev, jnp.max(s, axis=-1, keepdims=True))
                p = jnp.exp(s - m_new)
                alpha = jnp.exp(m_prev - m_new)
                l_new = alpha * l_prev + jnp.sum(p, axis=-1, keepdims=True)
                acc = alpha * acc + _dot(p.astype(BF16), vj)
                new.append((m_new, l_new, acc))
            return tuple(new)

        init = tuple((jnp.full((tile, 1), NEG_BIG, F32), jnp.zeros((tile, 1), F32),
                      jnp.zeros((tile, 2 * ATTN_HEAD_DIM), F32)) for _ in range(2))
        (_, l0, a0), (_, l1, a1) = lax.fori_loop(0, n_kv, kv_step, init)
        o = a0 / l0 - lam * (a1 / l1)
        o = o * lax.rsqrt(jnp.mean(o * o, axis=-1, keepdims=True) + SUBLN_EPS) * sg * (1.0 - lam_init)
        o_ref[:, hs] = o.astype(BF16)


def _flash(qb, kb, vb, lams, subln_g, n_seq, seq, lam_init):
    m, d = qb.shape
    tile = min(FLASH_TILE, seq)
    nq = seq // tile
    n_heads = d // (2 * ATTN_HEAD_DIM)
    lam_specs = [_const_spec((1, ATTN_HEAD_DIM))] * 4
    return pl.pallas_call(
        functools.partial(_flash_body, tile=tile, n_heads=n_heads, lam_init=lam_init),
        grid=(n_seq, nq),
        in_specs=[pl.BlockSpec((tile, d), lambda b, i: (b * nq + i, 0)),
                  pl.BlockSpec((seq, d), lambda b, i: (b, 0)),
                  pl.BlockSpec((seq, d), lambda b, i: (b, 0))] + lam_specs
                 + [_const_spec((1, 2 * ATTN_HEAD_DIM))],
        out_specs=pl.BlockSpec((tile, d), lambda b, i: (b * nq + i, 0)),
        out_shape=jax.ShapeDtypeStruct((m, d), BF16),
        compiler_params=_params(2),
        name="prompt_attention",
    )(qb, kb, vb, *lams, subln_g.reshape(1, -1))


def _decode_body(pt_ref, q_ref, kc_ref, vc_ref, kn_ref, vn_ref, slope_ref, madd_ref, tok_ref,
                 slope_n_ref, madd_n_ref, tok_n_ref, lq1, lk1, lq2, lk2, sg_ref, o_ref,
                 m_sc, l_sc, acc_sc, *, past_len, page, n_heads, dec_seq, lam_init):
    del pt_ref
    p = pl.program_id(1)
    n_pages = pl.num_programs(1)

    @pl.when(p == 0)
    def _():
        m_sc[...] = jnp.full(m_sc.shape, NEG_BIG, F32)
        l_sc[...] = jnp.zeros(l_sc.shape, F32)
        acc_sc[...] = jnp.zeros(acc_sc.shape, F32)

    q = q_ref[...]

    def update(kf, vf, bias):
        s = _dot_nt(q, kf.astype(BF16)) + bias
        m_prev = m_sc[...]
        m_new = jnp.maximum(m_prev, jnp.max(s, axis=-1, keepdims=True))
        pm = jnp.exp(s - m_new)
        alpha = jnp.exp(m_prev - m_new)
        l_sc[...] = alpha * l_sc[...] + jnp.sum(pm, axis=-1, keepdims=True)
        acc_sc[...] = alpha * acc_sc[...] + _dot(pm.astype(BF16), vf.astype(BF16))
        m_sc[...] = m_new

    rows = page * n_heads
    rel = tok_ref[...] + (p * page - past_len).astype(F32)
    update(kc_ref[...].reshape(rows, 2 * ATTN_HEAD_DIM), vc_ref[...].reshape(rows, 2 * ATTN_HEAD_DIM),
           slope_ref[...] * rel + madd_ref[...])

    @pl.when(p == n_pages - 1)
    def _():
        rows_n = kn_ref.shape[0] * n_heads
        update(kn_ref[...].reshape(rows_n, 2 * ATTN_HEAD_DIM), vn_ref[...].reshape(rows_n, 2 * ATTN_HEAD_DIM),
               slope_n_ref[...] * tok_n_ref[...] + madd_n_ref[...])
        lam = _lambda(lq1, lk1, lq2, lk2, lam_init)
        an = acc_sc[...] / l_sc[...]
        n_rows = an.shape[0]
        o = an - lam * pltpu.roll(an, n_rows - dec_seq, 0)
        o = o * lax.rsqrt(jnp.mean(o * o, axis=-1, keepdims=True) + SUBLN_EPS) * sg_ref[...] * (1.0 - lam_init)
        o_ref[...] = o


def _decode(q_all, cache_k, cache_v, layer, page_table, k_new, v_new, lams, subln_g, lam_init):
    db, n_rows, hd = q_all.shape
    _, _, page, n_heads, _ = cache_k.shape
    n_pages = page_table.shape[1]
    past_len = n_pages * page
    dec_seq = n_rows // (2 * n_heads)
    new_pad = k_new.shape[1]
    slopes = [2.0 ** (-8.0 * (h + 1) / n_heads) for h in range(n_heads)]

    def consts(n_tok, causal):
        r = jnp.arange(n_rows)
        c = jnp.arange(n_tok * n_heads)
        r_head, r_tok = r // (2 * dec_seq), r % dec_seq
        c_tok, c_head = c // n_heads, c % n_heads
        ok = r_head[:, None] == c_head[None, :]
        if causal:
            ok = ok & (c_tok[None, :] <= r_tok[:, None]) & (c_tok[None, :] < dec_seq)
        slope = jnp.asarray(slopes, F32)[r_head][:, None] * jnp.ones((1, c.shape[0]), F32)
        return slope, jnp.where(ok, 0.0, NEG_BIG).astype(F32), c_tok.astype(F32)[None, :]

    slope_p, madd_p, tok_p = consts(page, False)
    slope_n, madd_n, tok_n = consts(new_pad, True)
    cols_p, cols_n = page * n_heads, new_pad * n_heads
    cache_spec = pl.BlockSpec((None, None, page, n_heads, hd), lambda b, p, pt: (layer, pt[b, p], 0, 0, 0))
    new_spec = pl.BlockSpec((None, new_pad, n_heads, hd), lambda b, p, pt: (b, 0, 0, 0))

    def cst(shape):
        nd = len(shape)
        return pl.BlockSpec(shape, lambda b, p, pt: (0,) * nd)

    grid_spec = pltpu.PrefetchScalarGridSpec(
        num_scalar_prefetch=1,
        grid=(db, n_pages),
        in_specs=[pl.BlockSpec((None, n_rows, hd), lambda b, p, pt: (b, 0, 0)),
                  cache_spec, cache_spec, new_spec, new_spec,
                  cst((n_rows, cols_p)), cst((n_rows, cols_p)), cst((1, cols_p)),
                  cst((n_rows, cols_n)), cst((n_rows, cols_n)), cst((1, cols_n))]
                 + [cst((1, ATTN_HEAD_DIM))] * 4 + [cst((1, hd))],
        out_specs=pl.BlockSpec((None, n_rows, hd), lambda b, p, pt: (b, 0, 0)),
        scratch_shapes=[pltpu.VMEM((n_rows, 1), F32), pltpu.VMEM((n_rows, 1), F32),
                        pltpu.VMEM((n_rows, hd), F32)],
    )
    return pl.pallas_call(
        functools.partial(_decode_body, past_len=past_len, page=page, n_heads=n_heads,
                          dec_seq=dec_seq, lam_init=lam_init),
        grid_spec=grid_spec,
        out_shape=jax.ShapeDtypeStruct((db, n_rows, hd), F32),
        compiler_params=_params(2),
        name="decode_attention",
    )(page_table, q_all, cache_k, cache_v, k_new, v_new, slope_p, madd_p, tok_p,
      slope_n, madd_n, tok_n, *lams, subln_g.reshape(1, -1))


def _proj_res_body(x_ref, a_ref, w_ref, o_ref):
    o_ref[...] = x_ref[...] + _dot(a_ref[...], w_ref[...])


def _proj_res(x, a_bf, w_bf, tm):
    m, d = x.shape
    return pl.pallas_call(
        _proj_res_body,
        grid=(m // tm,),
        in_specs=[_row_spec(tm, d), _row_spec(tm, d), _const_spec((d, d))],
        out_specs=_row_spec(tm, d),
        out_shape=jax.ShapeDtypeStruct((m, d), F32),
        compiler_params=_params(1),
        name="attn_out_proj",
    )(x, a_bf, w_bf)


def _shifted(h, prev, stride):
    tm = h.shape[0]
    if stride % 8 == 0:
        return jnp.concatenate([prev, h[:tm - stride]], axis=0)
    assert stride == 1
    row = lax.broadcasted_iota(jnp.int32, h.shape, 0)
    return jnp.where(row == 0, prev, pltpu.roll(h, 1, 0))


def _rwkv_proj_body(x_ref, g_ref, sh0_ref, mix_ref, wrkv_ref, w0_ref, d1_ref, d2_ref, a0_ref, i1_ref, i2_ref,
                    g1_ref, g2_ref, kk_ref, ka_ref, rk_ref, pd_ref, pu_ref,
                    r_out, lw_out, k_out, v_out, nkk_out, kka_out, g_out, bonus_out, sh_out,
                    carry_sc, *, stride, tiles_per_seq):
    i = pl.program_id(0)
    h = _rms(x_ref[...], g_ref[...])
    tm = h.shape[0]

    @pl.when(i % tiles_per_seq == 0)
    def _():
        carry_sc[...] = sh0_ref[...]

    hp = _shifted(h, carry_sc[...], stride)
    carry_sc[...] = h[tm - stride:]
    sh_out[...] = h[tm - stride:]
    xx = hp - h
    mix = mix_ref[...]

    def xm(s):
        return (h + xx * mix[s:s + 1]).astype(BF16)

    r = _dot(xm(0), wrkv_ref[0])
    k = _dot(xm(1), wrkv_ref[1])
    v = _dot(xm(2), wrkv_ref[2])
    lora_w = _dot(jnp.tanh(_dot(xm(3), d1_ref[...])).astype(BF16), d2_ref[...])
    z = -(w0_ref[...] + lora_w)
    softplus = jnp.maximum(z, 0.0) + jnp.log(1.0 + jnp.exp(-jnp.abs(z)))
    lw_out[...] = -jnp.exp(-softplus - 0.5)
    a = jax.nn.sigmoid(a0_ref[...] + _dot(_dot(xm(4), i1_ref[...]).astype(BF16), i2_ref[...]))
    g_out[...] = _dot(jax.nn.sigmoid(_dot(xm(5), g1_ref[...])).astype(BF16), g2_ref[...])
    pd = pd_ref[...]
    pu = pu_ref[...]
    kk = k * kk_ref[...]
    kk = kk / jnp.maximum(jnp.sqrt(_seg_sum(kk * kk, pd, pu)), 1e-12)
    k = k * (1.0 + (a - 1.0) * ka_ref[...])
    r_out[...] = r
    k_out[...] = k
    v_out[...] = v
    nkk_out[...] = -kk
    kka_out[...] = kk * a
    bonus_out[...] = _seg_sum(r * k * rk_ref[...], pd, pu) * v


def _rwkv_proj(x, norm_g, shift0, w, tm, stride, tiles_per_seq):
    m, d = x.shape
    n_seq_blocks = shift0.shape[0]
    row = _row_spec(tm, d)
    f = jax.ShapeDtypeStruct((m, d), F32)
    vec = _const_spec((1, d))
    seq_spec = pl.BlockSpec((None, stride, d), lambda i: (i // tiles_per_seq, 0, 0))
    lora = w["d1"].shape[1]
    gate = w["g1"].shape[1]
    return pl.pallas_call(
        functools.partial(_rwkv_proj_body, stride=stride, tiles_per_seq=tiles_per_seq),
        grid=(m // tm,),
        in_specs=[row, vec, seq_spec, _const_spec((6, d)), _const_spec((3, d, d)),
                  vec, _const_spec((d, lora)), _const_spec((lora, d)),
                  vec, _const_spec((d, lora)), _const_spec((lora, d)),
                  _const_spec((d, gate)), _const_spec((gate, d)),
                  vec, vec, vec, _const_spec((d, LANES)), _const_spec((LANES, d))],
        out_specs=[row] * 8 + [seq_spec],
        out_shape=[f] * 8 + [jax.ShapeDtypeStruct((n_seq_blocks, stride, d), F32)],
        scratch_shapes=[pltpu.VMEM((stride, d), F32)],
        compiler_params=_params(1),
        name="rwkv_proj",
    )(x, norm_g.reshape(1, d), shift0, w["mix"], w["rkv"], w["w0"], w["d1"], w["d2"], w["a0"], w["i1"], w["i2"],
      w["g1"], w["g2"], w["k_k"], w["k_a"], w["r_k"], w["pd"], w["pu"])


def _wkv_body(r_ref, lw_ref, k_ref, v_ref, a_ref, b_ref, s0_ref, y_ref, sout_ref, s_sc, *, n_heads):
    c = pl.program_id(1)
    n_chunks = pl.num_programs(1)
    L = r_ref.shape[0]
    N = RWKV_HEAD

    @pl.when(c == 0)
    def _():
        s_sc[...] = s0_ref[...]

    row = lax.broadcasted_iota(jnp.int32, (L, L), 0)
    col = lax.broadcasted_iota(jnp.int32, (L, L), 1)
    incl = row >= col
    strict = row > col
    tri = incl.astype(F32).astype(BF16)
    eye = (row == col).astype(F32)

    lw = lw_ref[...]
    hi = lw.astype(BF16)
    r1 = lw - hi.astype(F32)
    mid = r1.astype(BF16)
    lo = (r1 - mid.astype(F32)).astype(BF16)
    cum = _dot(tri, hi) + _dot(tri, mid) + _dot(tri, lo)
    wl = cum[L - 1:L]
    e_in = jnp.exp(cum)
    e_out = jnp.exp(-cum)
    e_tail = jnp.exp(wl - cum)
    at = (a_ref[...] * jnp.exp(cum - lw)).astype(BF16)
    rt = (r_ref[...] * e_in).astype(BF16)
    bt = (b_ref[...] * e_out).astype(BF16)
    kt = (k_ref[...] * e_out).astype(BF16)
    bw = (b_ref[...] * e_tail).astype(BF16)
    kw = (k_ref[...] * e_tail).astype(BF16)
    vb = v_ref[...].astype(BF16)
    ewl = jnp.exp(wl)

    n_doublings = int(math.log2(L)) - 1
    for h in range(n_heads):
        hs = slice(h * N, (h + 1) * N)
        ah, rh, bh, kh, vh = at[:, hs], rt[:, hs], bt[:, hs], kt[:, hs], vb[:, hs]
        s_h = s_sc[h]
        s_hb = s_h.astype(BF16)
        a_ab = jnp.where(strict, _dot_nt(ah, bh), 0.0)
        a_ak = jnp.where(strict, _dot_nt(ah, kh), 0.0)
        a_rb = jnp.where(incl, _dot_nt(rh, bh), 0.0)
        a_rk = jnp.where(incl, _dot_nt(rh, kh), 0.0)
        t_inv = eye + a_ab
        a_pow = a_ab
        for _ in range(n_doublings):
            a_pb = a_pow.astype(BF16)
            a_pow = _dot(a_pb, a_pb)
            t_inv = t_inv + _dot(t_inv.astype(BF16), a_pow.astype(BF16))
        p_a = _dot_nt(ah, s_hb)
        p_r = _dot_nt(rh, s_hb)
        u = _dot(t_inv.astype(BF16), (p_a + _dot(a_ak.astype(BF16), vh)).astype(BF16))
        ub = u.astype(BF16)
        y_ref[:, hs] = p_r + _dot(a_rb.astype(BF16), ub) + _dot(a_rk.astype(BF16), vh)
        s_sc[h] = s_h * ewl[:, hs] + _dot_tn(ub, bw[:, hs]) + _dot_tn(vh, kw[:, hs])

    @pl.when(c == n_chunks - 1)
    def _():
        sout_ref[...] = s_sc[...]


def _wkv(r, lw, k, v, a, b, s0, chunk):
    m, d = r.shape
    n_seq, n_heads = s0.shape[0], s0.shape[1]
    n_chunks = m // (n_seq * chunk)
    row = pl.BlockSpec((chunk, d), lambda s, c: (s * n_chunks + c, 0))
    st = pl.BlockSpec((None, n_heads, RWKV_HEAD, RWKV_HEAD), lambda s, c: (s, 0, 0, 0))
    return pl.pallas_call(
        functools.partial(_wkv_body, n_heads=n_heads),
        grid=(n_seq, n_chunks),
        in_specs=[row] * 6 + [st],
        out_specs=[row, st],
        out_shape=[jax.ShapeDtypeStruct((m, d), F32), jax.ShapeDtypeStruct(s0.shape, F32)],
        scratch_shapes=[pltpu.VMEM((n_heads, RWKV_HEAD, RWKV_HEAD), F32)],
        compiler_params=_params(2),
        name="wkv7_chunked",
    )(r, lw, k, v, a, b, s0)


def _rwkv_out_body(x_ref, y_ref, bonus_ref, g_ref, lg_ref, lb_ref, pd_ref, pu_ref, w_ref, o_ref):
    pd = pd_ref[...]
    pu = pu_ref[...]
    y = y_ref[...]
    inv_n = 1.0 / RWKV_HEAD
    yc = y - _seg_sum(y, pd, pu) * inv_n
    var = _seg_sum(yc * yc, pd, pu) * inv_n
    yn = yc * lax.rsqrt(var + LNX_EPS) * lg_ref[...] + lb_ref[...]
    o_ref[...] = x_ref[...] + _dot(((yn + bonus_ref[...]) * g_ref[...]).astype(BF16), w_ref[...])


def _rwkv_out(x, y, bonus, g, w, tm):
    m, d = x.shape
    row = _row_spec(tm, d)
    vec = _const_spec((1, d))
    return pl.pallas_call(
        _rwkv_out_body,
        grid=(m // tm,),
        in_specs=[row, row, row, row, vec, vec, _const_spec((d, LANES)), _const_spec((LANES, d)),
                  _const_spec((d, d))],
        out_specs=row,
        out_shape=jax.ShapeDtypeStruct((m, d), F32),
        compiler_params=_params(1),
        name="rwkv_out_proj",
    )(x, y, bonus, g, w["lnx_g"], w["lnx_b"], w["pd"], w["pu"], w["wo"])


def _ffn_body(*refs, stride, tiles_per_seq, n_chunks, has_buf, final):
    x_ref, g_ref, wg_ref, wu_ref, cw_ref, cb_ref, wd_ref = refs[:7]
    pos = 7
    buf_ref = gf_ref = None
    if has_buf:
        buf_ref = refs[pos]
        pos += 1
    if final:
        gf_ref = refs[pos]
        pos += 1
    o_ref, cv_ref, carry_sc, acc_sc = refs[pos:pos + 4]
    i = pl.program_id(0)
    x = x_ref[...]
    tm = x.shape[0]
    fc = wg_ref.shape[2]
    h = _rms(x, g_ref[...]).astype(BF16)
    acc_sc[...] = jnp.zeros(acc_sc.shape, F32)
    if not has_buf:
        @pl.when(i % tiles_per_seq == 0)
        def _():
            carry_sc[...] = jnp.zeros(carry_sc.shape, F32)
        row = lax.broadcasted_iota(jnp.int32, (tm, fc), 0)

    def chunk(c, carry):
        u = _dot(h, wg_ref[c])
        up = _dot(h, wu_ref[c])
        cw = cw_ref[c]
        if has_buf:
            full = jnp.concatenate([buf_ref[c], u], axis=0)
            u2 = full[0:tm]
            u1 = full[stride:stride + tm]
            cv_ref[c] = full[tm:tm + 2 * stride]
        else:
            prev = carry_sc[c]
            u1 = jnp.where(row == 0, prev[1:2], pltpu.roll(u, 1, 0))
            u2 = jnp.where(row == 0, prev[0:1], jnp.where(row == 1, prev[1:2], pltpu.roll(u, 2, 0)))
            carry_sc[c] = u[tm - 2:tm]
            cv_ref[c] = u[tm - 2:tm]
        cv = cb_ref[c] + u2 * cw[0:1] + u1 * cw[1:2] + u * cw[2:3]
        act = (cv * jax.nn.sigmoid(cv) * up).astype(BF16)
        acc_sc[...] += _dot(act, wd_ref[c])
        return carry

    lax.fori_loop(0, n_chunks, chunk, 0)
    out = x + acc_sc[...]
    if final:
        out = _rms(out, gf_ref[...])
    o_ref[...] = out


def _ffn(x, norm_g, w, tm, stride, tiles_per_seq, buf0, final_g):
    m, d = x.shape
    n_chunks, _, fc = w["wg"].shape
    has_buf = buf0 is not None
    final = final_g is not None
    n_seq_blocks = (m // tm) // tiles_per_seq
    keep = 2 * stride
    seq_spec = pl.BlockSpec((None, n_chunks, keep, fc), lambda i: (i // tiles_per_seq, 0, 0, 0))
    ins = [x, norm_g.reshape(1, d), w["wg"], w["wu"], w["cw"], w["cb"], w["wd"]]
    in_specs = [_row_spec(tm, d), _const_spec((1, d)), _const_spec((n_chunks, d, fc)),
                _const_spec((n_chunks, d, fc)), _const_spec((n_chunks, 3, fc)), _const_spec((n_chunks, 1, fc)),
                _const_spec((n_chunks, fc, d))]
    if has_buf:
        ins.append(buf0)
        in_specs.append(seq_spec)
    if final:
        ins.append(final_g.reshape(1, d))
        in_specs.append(_const_spec((1, d)))
    return pl.pallas_call(
        functools.partial(_ffn_body, stride=stride, tiles_per_seq=tiles_per_seq, n_chunks=n_chunks,
                          has_buf=has_buf, final=final),
        grid=(m // tm,),
        in_specs=in_specs,
        out_specs=[_row_spec(tm, d), seq_spec],
        out_shape=[jax.ShapeDtypeStruct((m, d), F32),
                   jax.ShapeDtypeStruct((n_seq_blocks, n_chunks, keep, fc), F32)],
        scratch_shapes=[pltpu.VMEM((n_chunks, 2, fc), F32), pltpu.VMEM((tm, d), F32)],
        compiler_params=_params(1),
        name="conv_ffn",
    )(*ins)


def _pad_cols(w, n):
    return jnp.pad(w, ((0, 0), (0, n - w.shape[1])))


def _pad_rows(w, n):
    return jnp.pad(w, ((0, n - w.shape[0]), (0, 0)))


def _round_up(n, mult):
    return -(-n // mult) * mult


def _head_maps(d, head):
    onehot = (jnp.arange(d)[:, None] // head == jnp.arange(LANES)[None, :]).astype(BF16)
    return onehot, onehot.T


def _row_tile(rows):
    tm = min(ROW_TILE, rows)
    assert rows % tm == 0
    return tm


def kernel(x_prompt, x_sample, cache_k, cache_v, page_table, state_shift, state_wkv, state_ffn_conv, attn_norm_g, w_qkv, lambda_q1, lambda_k1, lambda_q2, lambda_k2, subln_g, w_o_attn, rwkv_norm_g, rwkv_mix, w_rkv, w_decay0, w_decay1, w_decay2, w_iclr0, w_iclr1, w_iclr2, rwkv_g1, rwkv_g2, k_k, k_a, r_k, lnx_g, lnx_b, w_o_rwkv, ffn_norm_g, ffn_w_gate, ffn_w_up, ffn_conv_w, ffn_conv_b, ffn_w_down, final_norm_g):
    B, T, D = x_prompt.shape
    DB, TS, _ = x_sample.shape
    depth = ffn_norm_g.shape[0]
    F = ffn_w_gate.shape[2]
    n_attn_heads = D // (2 * ATTN_HEAD_DIM)
    n_rwkv_heads = D // RWKV_HEAD
    hd = 2 * ATTN_HEAD_DIM
    tm_p = _row_tile(T)
    tps_p = T // tm_p
    ms = TS * DB
    fc = FFN_CHUNK
    n_chunks = F // fc
    assert F % fc == 0 and T % WKV_CHUNK == 0 and TS <= WKV_CHUNK and DB % 8 == 0

    xp = x_prompt.reshape(B * T, D)
    xs = x_sample.transpose(1, 0, 2).reshape(ms, D)
    pd, pu = _head_maps(D, RWKV_HEAD)

    kp_l, vp_l, ks_l, vs_l = [], [], [], []
    shp_l, shs_l, wkp_l, wks_l = [], [], [], []
    cvp_l, cvs_l = [], []
    for i in range(depth):
        j = i // 2
        if i % 2 == 0:
            lam_init = 0.8 - 0.6 * math.exp(-0.3 * i)
            lams = [v[j].reshape(1, -1) for v in (lambda_q1, lambda_k1, lambda_q2, lambda_k2)]
            wqkv = w_qkv[j].astype(BF16)
            wo = w_o_attn[j].astype(BF16)
            qb, kb, vb, k32, v32 = _qkv(xp, attn_norm_g[j], wqkv, tm_p)
            o = _flash(qb, kb, vb, lams, subln_g[j], B, T, lam_init)
            xp = _proj_res(xp, o, wo, tm_p)
            kp_l.append(k32.reshape(B, T, n_attn_heads, hd))
            vp_l.append(v32.reshape(B, T, n_attn_heads, hd))
            qb, _, _, k32, v32 = _qkv(xs, attn_norm_g[j], wqkv, ms)
            k_new = k32.reshape(TS, DB, n_attn_heads, hd).transpose(1, 0, 2, 3)
            v_new = v32.reshape(TS, DB, n_attn_heads, hd).transpose(1, 0, 2, 3)
            ks_l.append(k_new)
            vs_l.append(v_new)
            q5 = qb.reshape(TS, DB, n_attn_heads, 2, ATTN_HEAD_DIM).transpose(1, 2, 3, 0, 4)
            q_all = (q5[:, :, :, :, None, :] * jnp.eye(2, dtype=BF16)[None, None, :, None, :, None])
            q_all = q_all.reshape(DB, n_attn_heads * 2 * TS, hd)
            new_pad = _round_up(TS, 16)
            pad = ((0, 0), (0, new_pad - TS), (0, 0), (0, 0))
            o = _decode(q_all, cache_k, cache_v, j, page_table, jnp.pad(k_new, pad), jnp.pad(v_new, pad),
                        lams, subln_g[j], lam_init)
            o = o.reshape(DB, n_attn_heads, 2, TS, hd)[:, :, 0].transpose(2, 0, 1, 3).reshape(ms, D)
            xs = _proj_res(xs, o.astype(BF16), wo, ms)
        else:
            lora = _round_up(w_decay1.shape[2], LANES)
            gate = _round_up(rwkv_g1.shape[2], LANES)
            w = dict(
                mix=rwkv_mix[j], rkv=w_rkv[j].astype(BF16),
                w0=w_decay0[j].reshape(1, D), d1=_pad_cols(w_decay1[j], lora).astype(BF16),
                d2=_pad_rows(w_decay2[j], lora).astype(BF16),
                a0=w_iclr0[j].reshape(1, D), i1=_pad_cols(w_iclr1[j], lora).astype(BF16),
                i2=_pad_rows(w_iclr2[j], lora).astype(BF16),
                g1=_pad_cols(rwkv_g1[j], gate).astype(BF16), g2=_pad_rows(rwkv_g2[j], gate).astype(BF16),
                k_k=k_k[j].reshape(1, D), k_a=k_a[j].reshape(1, D), r_k=r_k[j].reshape(1, D),
                lnx_g=lnx_g[j].reshape(1, D), lnx_b=lnx_b[j].reshape(1, D),
                wo=w_o_rwkv[j].astype(BF16), pd=pd, pu=pu)
            outs = _rwkv_proj(xp, rwkv_norm_g[j], jnp.zeros((B, 1, D), F32), w, tm_p, 1, tps_p)
            r, lw, k, v, nkk, kka, g, bonus, sh = outs
            y, s_fin = _wkv(r, lw, k, v, nkk, kka, jnp.zeros((B, n_rwkv_heads, RWKV_HEAD, RWKV_HEAD), F32),
                            WKV_CHUNK)
            xp = _rwkv_out(xp, y, bonus, g, w, tm_p)
            shp_l.append(sh.reshape(B, D))
            wkp_l.append(s_fin)
            outs = _rwkv_proj(xs, rwkv_norm_g[j], state_shift[j][None], w, ms, DB, 1)
            r, lw, k, v, nkk, kka, g, bonus, sh = outs

            def to_seq(a):
                a = a.reshape(TS, DB, D).transpose(1, 0, 2)
                return jnp.pad(a, ((0, 0), (0, WKV_CHUNK - TS), (0, 0))).reshape(DB * WKV_CHUNK, D)

            y, s_fin = _wkv(*(to_seq(a) for a in (r, lw, k, v, nkk, kka)), state_wkv[j], WKV_CHUNK)
            y = y.reshape(DB, WKV_CHUNK, D)[:, :TS].transpose(1, 0, 2).reshape(ms, D)
            xs = _rwkv_out(xs, y, bonus, g, w, ms)
            shs_l.append(sh.reshape(DB, D))
            wks_l.append(s_fin)
        w = dict(
            wg=ffn_w_gate[i].astype(BF16).reshape(D, n_chunks, fc).transpose(1, 0, 2),
            wu=ffn_w_up[i].astype(BF16).reshape(D, n_chunks, fc).transpose(1, 0, 2),
            cw=ffn_conv_w[i].reshape(3, n_chunks, fc).transpose(1, 0, 2),
            cb=ffn_conv_b[i].reshape(n_chunks, 1, fc),
            wd=ffn_w_down[i].astype(BF16).reshape(n_chunks, fc, D))
        final_g = final_norm_g if i == depth - 1 else None
        xp, cv = _ffn(xp, ffn_norm_g[i], w, tm_p, 1, tps_p, None, final_g)
        cvp_l.append(cv.transpose(0, 2, 1, 3).reshape(B, 2, F))
        buf0 = state_ffn_conv[i].reshape(DB, 2, n_chunks, fc).transpose(2, 1, 0, 3).reshape(1, n_chunks, 2 * DB, fc)
        xs, cv = _ffn(xs, ffn_norm_g[i], w, ms, DB, 1, buf0, final_g)
        cvs_l.append(cv.reshape(n_chunks, 2, DB, fc).transpose(2, 1, 0, 3).reshape(DB, 2, F))

    y_prompt = xp.reshape(B, T, D)
    y_sample = xs.reshape(TS, DB, D).transpose(1, 0, 2)
    return (y_prompt, y_sample,
            jnp.stack(kp_l), jnp.stack(vp_l), jnp.stack(ks_l), jnp.stack(vs_l),
            jnp.stack(shp_l), jnp.stack(shs_l), jnp.stack(wkp_l), jnp.stack(wks_l),
            jnp.stack(cvp_l), jnp.stack(cvs_l))
```

```python
import functools
import math

import jax
import jax.numpy as jnp
from jax import lax
from jax.experimental import pallas as pl
from jax.experimental.pallas import tpu as pltpu

F32 = jnp.float32
BF16 = jnp.bfloat16

RMS_EPS = 1e-6
SUBLN_EPS = 1e-5
LNX_EPS = 64e-5
ATTN_HEAD_DIM = 64
RWKV_HEAD = 64
LANES = 128
NEG_BIG = -1e30
LOG2E = 1.4426950408889634
VMEM_LIMIT_BYTES = 56 * 1024 * 1024

ROW_TILE = 512
FFN_ROW_TILE = 1024
FLASH_TILE = 256
FFN_CHUNK = 256
WKV_CHUNK = 64

NT_DIMS = (((1,), (1,)), ((), ()))
TN_DIMS = (((0,), (0,)), ((), ()))


def _params(n_axes):
    return pltpu.CompilerParams(dimension_semantics=("arbitrary",) * n_axes,
                                vmem_limit_bytes=VMEM_LIMIT_BYTES)


def _dot(a, b):
    return jnp.dot(a, b, preferred_element_type=F32)


def _dot_nt(a, b):
    return lax.dot_general(a, b, NT_DIMS, preferred_element_type=F32)


def _dot_tn(a, b):
    return lax.dot_general(a, b, TN_DIMS, preferred_element_type=F32)


def _rms(x, g, eps=RMS_EPS):
    return x * lax.rsqrt(jnp.mean(x * x, axis=-1, keepdims=True) + eps) * g


def _const_spec(shape):
    nd = len(shape)
    return pl.BlockSpec(shape, lambda *_: (0,) * nd, pipeline_mode=pl.Buffered(1))


def _row_spec(tm, d):
    return pl.BlockSpec((tm, d), lambda i: (i, 0))


def _seg_sum(x, pd, pu):
    hi = x.astype(BF16)
    lo = (x - hi.astype(F32)).astype(BF16)
    s = _dot(hi, pd) + _dot(lo, pd)
    s1 = s.astype(BF16)
    r = s - s1.astype(F32)
    s2 = r.astype(BF16)
    s3 = (r - s2.astype(F32)).astype(BF16)
    return _dot(s1, pu) + _dot(s2, pu) + _dot(s3, pu)


def _tile_lanes(x, width):
    if width <= LANES:
        return x[:, :width]
    return jnp.concatenate([x] * (width // LANES), axis=1)


def _lambda(lq1, lk1, lq2, lk2, lam_init):
    s1 = jnp.sum(lq1[...] * lk1[...], axis=-1, keepdims=True)
    s2 = jnp.sum(lq2[...] * lk2[...], axis=-1, keepdims=True)
    return jnp.exp(s1) - jnp.exp(s2) + lam_init


def _qkv_body(x_ref, g_ref, w_ref, q0_ref, q1_ref, kb_ref, vb_ref, k_ref, v_ref):
    d = x_ref.shape[1]
    h = _rms(x_ref[...], g_ref[...]).astype(BF16)
    q = (_dot(h, w_ref[:, 0:d]) * (ATTN_HEAD_DIM ** -0.5 * LOG2E)).astype(BF16)
    first_map = (lax.broadcasted_iota(jnp.int32, q.shape, 1) // ATTN_HEAD_DIM) % 2 == 0
    zero = jnp.zeros_like(q)
    q0_ref[...] = jnp.where(first_map, q, zero)
    q1_ref[...] = jnp.where(first_map, zero, q)
    k = _dot(h, w_ref[:, d:2 * d])
    k_ref[...] = k
    kb_ref[...] = k.astype(BF16)
    v = _dot(h, w_ref[:, 2 * d:3 * d])
    v_ref[...] = v
    vb_ref[...] = v.astype(BF16)


def _qkv(x, g, w_bf, tm):
    m, d = x.shape
    bf = jax.ShapeDtypeStruct((m, d), BF16)
    f = jax.ShapeDtypeStruct((m, d), F32)
    return pl.pallas_call(
        _qkv_body,
        grid=(m // tm,),
        in_specs=[_row_spec(tm, d), _const_spec((1, d)), _const_spec((d, 3 * d))],
        out_specs=[_row_spec(tm, d)] * 6,
        out_shape=[bf, bf, bf, bf, f, f],
        compiler_params=_params(1),
        name="qkv_proj",
    )(x, g.reshape(1, d), w_bf)


def _flash_body(q0_ref, q1_ref, k_ref, v_ref, lq1, lk1, lq2, lk2, sg_ref, o_ref, m_sc, l_sc, acc_sc,
                *, tile, n_heads, lam_init):
    qi = pl.program_id(1)
    q_start = qi * tile
    hd = 2 * ATTN_HEAD_DIM
    q_refs = (q0_ref, q1_ref)
    m_sc[...] = jnp.full(m_sc.shape, NEG_BIG, F32)
    l_sc[...] = jnp.zeros(l_sc.shape, F32)
    acc_sc[...] = jnp.zeros(acc_sc.shape, F32)
    col_row = lax.broadcasted_iota(jnp.int32, (1, tile), 1)

    def kv_step(j, on_diagonal):
        k0 = pl.multiple_of(j * tile, tile)
        rel = (k0 - q_start + col_row).astype(F32)
        if on_diagonal:
            keep = (lax.broadcasted_iota(jnp.int32, (tile, tile), 1)
                    <= lax.broadcasted_iota(jnp.int32, (tile, tile), 0))
        for h in range(n_heads):
            hs = slice(h * hd, (h + 1) * hd)
            bias = (LOG2E * 2.0 ** (-8.0 * (h + 1) / n_heads)) * rel
            kj = k_ref[pl.ds(k0, tile), hs]
            vj = v_ref[pl.ds(k0, tile), hs]
            for c in range(2):
                idx = 2 * h + c
                s = _dot_nt(q_refs[c][:, hs], kj) + bias
                if on_diagonal:
                    s = jnp.where(keep, s, NEG_BIG)
                m_prev = m_sc[idx]
                m_new = jnp.maximum(m_prev, jnp.max(s, axis=-1, keepdims=True))
                p = jnp.exp2(s - _tile_lanes(m_new, tile))
                alpha = jnp.exp2(m_prev - m_new)
                l_sc[idx] = alpha * l_sc[idx] + jnp.sum(p, axis=-1, keepdims=True)
                acc_sc[idx] = alpha * acc_sc[idx] + _dot(p.astype(BF16), vj)
                m_sc[idx] = m_new

    def off_diagonal(j, carry):
        kv_step(j, False)
        return carry

    lax.fori_loop(0, qi, off_diagonal, 0)
    kv_step(qi, True)

    lam = _lambda(lq1, lk1, lq2, lk2, lam_init)
    sg = sg_ref[...]
    for h in range(n_heads):
        o = acc_sc[2 * h] / l_sc[2 * h] - lam * (acc_sc[2 * h + 1] / l_sc[2 * h + 1])
        o = o * lax.rsqrt(jnp.mean(o * o, axis=-1, keepdims=True) + SUBLN_EPS) * sg * (1.0 - lam_init)
        o_ref[:, h * hd:(h + 1) * hd] = o.astype(BF16)


def _flash(q0b, q1b, kb, vb, lams, subln_g, n_seq, seq, lam_init):
    m, d = q0b.shape
    tile = min(FLASH_TILE, seq)
    nq = seq // tile
    hd = 2 * ATTN_HEAD_DIM
    n_heads = d // hd
    q_spec = pl.BlockSpec((tile, d), lambda b, i: (b * nq + i, 0))
    kv_spec = pl.BlockSpec((seq, d), lambda b, i: (b, 0))
    return pl.pallas_call(
        functools.partial(_flash_body, tile=tile, n_heads=n_heads, lam_init=lam_init),
        grid=(n_seq, nq),
        in_specs=[q_spec, q_spec, kv_spec, kv_spec] + [_const_spec((1, ATTN_HEAD_DIM))] * 4
                 + [_const_spec((1, hd))],
        out_specs=q_spec,
        out_shape=jax.ShapeDtypeStruct((m, d), BF16),
        scratch_shapes=[pltpu.VMEM((2 * n_heads, tile, LANES), F32), pltpu.VMEM((2 * n_heads, tile, LANES), F32),
                        pltpu.VMEM((2 * n_heads, tile, hd), F32)],
        compiler_params=_params(2),
        name="prompt_attention",
    )(q0b, q1b, kb, vb, *lams, subln_g.reshape(1, -1))


def _decode_body(*refs, past_len, page, n_heads, dec_seq, pages_per_step, lam_init):
    npp = pages_per_step
    q_ref = refs[1]
    kc_refs = refs[2:2 + npp]
    vc_refs = refs[2 + npp:2 + 2 * npp]
    (kn_ref, vn_ref, slope_ref, madd_ref, tok_ref, slope_n_ref, madd_n_ref, tok_n_ref,
     lq1, lk1, lq2, lk2, sg_ref, o_ref, m_sc, l_sc, acc_sc) = refs[2 + 2 * npp:]
    p = pl.program_id(1)
    n_steps = pl.num_programs(1)
    hd = 2 * ATTN_HEAD_DIM

    @pl.when(p == 0)
    def _():
        m_sc[...] = jnp.full(m_sc.shape, NEG_BIG, F32)
        l_sc[...] = jnp.zeros(l_sc.shape, F32)
        acc_sc[...] = jnp.zeros(acc_sc.shape, F32)

    q = q_ref[...]

    def update(k_list, v_list, base_bias, row_shift):
        s = [_dot_nt(q, kf.astype(BF16)) + base_bias for kf in k_list]
        m_prev = m_sc[...]
        m_new = m_prev
        for si, ri in zip(s, row_shift):
            m_new = jnp.maximum(m_new, jnp.max(si, axis=-1, keepdims=True) + ri)
        alpha = jnp.exp2(m_prev - m_new)
        l_new = alpha * l_sc[...]
        acc = alpha * acc_sc[...]
        for si, ri, vf in zip(s, row_shift, v_list):
            pm = jnp.exp2(si - _tile_lanes(m_new - ri, si.shape[1]))
            l_new = l_new + jnp.sum(pm, axis=-1, keepdims=True)
            acc = acc + _dot(pm.astype(BF16), vf.astype(BF16))
        l_sc[...] = l_new
        acc_sc[...] = acc
        m_sc[...] = m_new

    rows = page * n_heads
    slope_col = slope_ref[:, 0:LANES]
    rel = tok_ref[...] + (p * (npp * page) - past_len).astype(F32)
    update([r[...].reshape(rows, hd) for r in kc_refs], [r[...].reshape(rows, hd) for r in vc_refs],
           slope_ref[...] * rel + madd_ref[...], [slope_col * float(i * page) for i in range(npp)])

    @pl.when(p == n_steps - 1)
    def _():
        rows_n = kn_ref.shape[0] * n_heads
        update([kn_ref[...].reshape(rows_n, hd)], [vn_ref[...].reshape(rows_n, hd)],
               slope_n_ref[...] * tok_n_ref[...] + madd_n_ref[...], [0.0])
        lam = _lambda(lq1, lk1, lq2, lk2, lam_init)
        an = acc_sc[...] / l_sc[...]
        n_rows = an.shape[0]
        o = an - lam * pltpu.roll(an, n_rows - dec_seq, 0)
        o = o * lax.rsqrt(jnp.mean(o * o, axis=-1, keepdims=True) + SUBLN_EPS) * sg_ref[...] * (1.0 - lam_init)
        o_ref[...] = o


def _decode(q_all, cache_k, cache_v, layer, page_table, k_new, v_new, lams, subln_g, lam_init):
    db, n_rows, hd = q_all.shape
    _, _, page, n_heads, _ = cache_k.shape
    n_pages = page_table.shape[1]
    past_len = n_pages * page
    dec_seq = n_rows // (2 * n_heads)
    new_pad = k_new.shape[1]
    npp = max(c for c in (8, 4, 2, 1) if n_pages % c == 0)
    slopes = [LOG2E * 2.0 ** (-8.0 * (h + 1) / n_heads) for h in range(n_heads)]

    def consts(n_tok, causal):
        r = jnp.arange(n_rows)
        c = jnp.arange(n_tok * n_heads)
        r_head, r_tok = r // (2 * dec_seq), r % dec_seq
        c_tok, c_head = c // n_heads, c % n_heads
        ok = r_head[:, None] == c_head[None, :]
        if causal:
            ok = ok & (c_tok[None, :] <= r_tok[:, None]) & (c_tok[None, :] < dec_seq)
        slope = jnp.asarray(slopes, F32)[r_head][:, None] * jnp.ones((1, c.shape[0]), F32)
        return slope, jnp.where(ok, 0.0, NEG_BIG).astype(F32), c_tok.astype(F32)[None, :]

    slope_p, madd_p, tok_p = consts(page, False)
    slope_n, madd_n, tok_n = consts(new_pad, True)
    cols_p, cols_n = page * n_heads, new_pad * n_heads
    def cache_spec(i):
        return pl.BlockSpec((None, None, page, n_heads, hd),
                            lambda b, p, pt: (layer, pt[b, p * npp + i], 0, 0, 0))

    new_spec = pl.BlockSpec((None, new_pad, n_heads, hd), lambda b, p, pt: (b, 0, 0, 0))

    def cst(shape):
        nd = len(shape)
        return pl.BlockSpec(shape, lambda b, p, pt: (0,) * nd)

    grid_spec = pltpu.PrefetchScalarGridSpec(
        num_scalar_prefetch=1,
        grid=(db, n_pages // npp),
        in_specs=[pl.BlockSpec((None, n_rows, hd), lambda b, p, pt: (b, 0, 0))]
                 + [cache_spec(i) for i in range(npp)] * 2
                 + [new_spec, new_spec,
                    cst((n_rows, cols_p)), cst((n_rows, cols_p)), cst((1, cols_p)),
                    cst((n_rows, cols_n)), cst((n_rows, cols_n)), cst((1, cols_n))]
                 + [cst((1, ATTN_HEAD_DIM))] * 4 + [cst((1, hd))],
        out_specs=pl.BlockSpec((None, n_rows, hd), lambda b, p, pt: (b, 0, 0)),
        scratch_shapes=[pltpu.VMEM((n_rows, LANES), F32), pltpu.VMEM((n_rows, LANES), F32),
                        pltpu.VMEM((n_rows, hd), F32)],
    )
    return pl.pallas_call(
        functools.partial(_decode_body, past_len=past_len, page=page, n_heads=n_heads,
                          dec_seq=dec_seq, pages_per_step=npp, lam_init=lam_init),
        grid_spec=grid_spec,
        out_shape=jax.ShapeDtypeStruct((db, n_rows, hd), F32),
        compiler_params=_params(2),
        name="decode_attention",
    )(page_table, q_all, *([cache_k] * npp), *([cache_v] * npp), k_new, v_new, slope_p, madd_p, tok_p,
      slope_n, madd_n, tok_n, *lams, subln_g.reshape(1, -1))


def _proj_res_body(x_ref, a_ref, w_ref, o_ref):
    o_ref[...] = x_ref[...] + _dot(a_ref[...], w_ref[...])


def _proj_res(x, a_bf, w_bf, tm):
    m, d = x.shape
    return pl.pallas_call(
        _proj_res_body,
        grid=(m // tm,),
        in_specs=[_row_spec(tm, d), _row_spec(tm, d), _const_spec((d, d))],
        out_specs=_row_spec(tm, d),
        out_shape=jax.ShapeDtypeStruct((m, d), F32),
        compiler_params=_params(1),
        name="attn_out_proj",
    )(x, a_bf, w_bf)


def _shifted(h, prev, stride):
    tm = h.shape[0]
    if stride % 8 == 0:
        return jnp.concatenate([prev, h[:tm - stride]], axis=0)
    assert stride == 1
    row = lax.broadcasted_iota(jnp.int32, h.shape, 0)
    return jnp.where(row == 0, prev, pltpu.roll(h, 1, 0))


def _rwkv_proj_body(x_ref, g_ref, sh0_ref, mix_ref, wrkv_ref, w0_ref, d1_ref, d2_ref, a0_ref, i1_ref, i2_ref,
                    g1_ref, g2_ref, kk_ref, ka_ref, rk_ref, pd_ref, pu_ref,
                    r_out, lw_out, k_out, v_out, nkk_out, kka_out, g_out, bonus_out, sh_out,
                    carry_sc, *, stride, tiles_per_seq):
    i = pl.program_id(0)
    h = _rms(x_ref[...], g_ref[...])
    tm = h.shape[0]

    @pl.when(i % tiles_per_seq == 0)
    def _():
        carry_sc[...] = sh0_ref[...]

    hp = _shifted(h, carry_sc[...], stride)
    carry_sc[...] = h[tm - stride:]
    sh_out[...] = h[tm - stride:]
    xx = hp - h
    mix = mix_ref[...]

    def xm(s):
        return (h + xx * mix[s:s + 1]).astype(BF16)

    r = _dot(xm(0), wrkv_ref[0])
    k = _dot(xm(1), wrkv_ref[1])
    v = _dot(xm(2), wrkv_ref[2])
    lora_w = _dot(jnp.tanh(_dot(xm(3), d1_ref[...])).astype(BF16), d2_ref[...])
    z = -(w0_ref[...] + lora_w)
    softplus = jnp.maximum(z, 0.0) + jnp.log(1.0 + jnp.exp(-jnp.abs(z)))
    lw_out[...] = -jnp.exp(-softplus - 0.5)
    a = jax.nn.sigmoid(a0_ref[...] + _dot(_dot(xm(4), i1_ref[...]).astype(BF16), i2_ref[...]))
    g_out[...] = _dot(jax.nn.sigmoid(_dot(xm(5), g1_ref[...])).astype(BF16), g2_ref[...])
    pd = pd_ref[...]
    pu = pu_ref[...]
    kk = k * kk_ref[...]
    kk = kk / jnp.maximum(jnp.sqrt(_seg_sum(kk * kk, pd, pu)), 1e-12)
    k = k * (1.0 + (a - 1.0) * ka_ref[...])
    r_out[...] = r
    k_out[...] = k
    v_out[...] = v
    nkk_out[...] = -kk
    kka_out[...] = kk * a
    bonus_out[...] = _seg_sum(r * k * rk_ref[...], pd, pu) * v


def _rwkv_proj(x, norm_g, shift0, w, tm, stride, tiles_per_seq):
    m, d = x.shape
    n_seq_blocks = shift0.shape[0]
    row = _row_spec(tm, d)
    f = jax.ShapeDtypeStruct((m, d), F32)
    vec = _const_spec((1, d))
    seq_spec = pl.BlockSpec((None, stride, d), lambda i: (i // tiles_per_seq, 0, 0))
    lora = w["d1"].shape[1]
    gate = w["g1"].shape[1]
    return pl.pallas_call(
        functools.partial(_rwkv_proj_body, stride=stride, tiles_per_seq=tiles_per_seq),
        grid=(m // tm,),
        in_specs=[row, vec, seq_spec, _const_spec((6, d)), _const_spec((3, d, d)),
                  vec, _const_spec((d, lora)), _const_spec((lora, d)),
                  vec, _const_spec((d, lora)), _const_spec((lora, d)),
                  _const_spec((d, gate)), _const_spec((gate, d)),
                  vec, vec, vec, _const_spec((d, LANES)), _const_spec((LANES, d))],
        out_specs=[row] * 8 + [seq_spec],
        out_shape=[f] * 8 + [jax.ShapeDtypeStruct((n_seq_blocks, stride, d), F32)],
        scratch_shapes=[pltpu.VMEM((stride, d), F32)],
        compiler_params=_params(1),
        name="rwkv_proj",
    )(x, norm_g.reshape(1, d), shift0, w["mix"], w["rkv"], w["w0"], w["d1"], w["d2"], w["a0"], w["i1"], w["i2"],
      w["g1"], w["g2"], w["k_k"], w["k_a"], w["r_k"], w["pd"], w["pu"])


def _wkv_body(r_ref, lw_ref, k_ref, v_ref, a_ref, b_ref, s0_ref, y_ref, sout_ref, s_sc, *, n_heads):
    c = pl.program_id(1)
    n_chunks = pl.num_programs(1)
    L = r_ref.shape[0]
    N = RWKV_HEAD

    @pl.when(c == 0)
    def _():
        s_sc[...] = s0_ref[...]

    row = lax.broadcasted_iota(jnp.int32, (L, L), 0)
    col = lax.broadcasted_iota(jnp.int32, (L, L), 1)
    incl = row >= col
    strict = row > col
    tri = incl.astype(F32).astype(BF16)
    eye = (row == col).astype(F32)

    lw = lw_ref[...]
    hi = lw.astype(BF16)
    r1 = lw - hi.astype(F32)
    mid = r1.astype(BF16)
    lo = (r1 - mid.astype(F32)).astype(BF16)
    cum = _dot(tri, hi) + _dot(tri, mid) + _dot(tri, lo)
    wl = cum[L - 1:L]
    e_in = jnp.exp(cum)
    e_out = jnp.exp(-cum)
    e_tail = jnp.exp(wl - cum)
    at = (a_ref[...] * jnp.exp(cum - lw)).astype(BF16)
    rt = (r_ref[...] * e_in).astype(BF16)
    bt = (b_ref[...] * e_out).astype(BF16)
    kt = (k_ref[...] * e_out).astype(BF16)
    bw = (b_ref[...] * e_tail).astype(BF16)
    kw = (k_ref[...] * e_tail).astype(BF16)
    vb = v_ref[...].astype(BF16)
    ewl = jnp.exp(wl)

    heads = range(n_heads)
    hs = [slice(h * N, (h + 1) * N) for h in heads]
    s_old = [s_sc[h] for h in heads]
    ar = [jnp.concatenate([at[:, s], rt[:, s]], axis=0) for s in hs]
    g_b = [_dot_nt(ar[h], bt[:, hs[h]]) for h in heads]
    g_k = [_dot_nt(ar[h], kt[:, hs[h]]) for h in heads]
    p_s = [_dot_nt(ar[h], s_old[h].astype(BF16)) for h in heads]
    a_ab = [jnp.where(strict, g[:L], 0.0) for g in g_b]
    a_rb = [jnp.where(incl, g[L:], 0.0).astype(BF16) for g in g_b]
    a_ak = [jnp.where(strict, g[:L], 0.0).astype(BF16) for g in g_k]
    a_rk = [jnp.where(incl, g[L:], 0.0).astype(BF16) for g in g_k]
    rhs = [(p_s[h][:L] + _dot(a_ak[h], vb[:, hs[h]])).astype(BF16) for h in heads]

    t_inv = [eye + a for a in a_ab]
    a_pow = []
    for a in a_ab:
        ab = a.astype(BF16)
        a_pow.append(_dot(ab, ab))
    n_steps = int(math.log2(L)) - 1
    for step in range(n_steps):
        last = step == n_steps - 1
        for h in heads:
            pb = a_pow[h].astype(BF16)
            if last:
                t_inv[h] = t_inv[h] + _dot(t_inv[h].astype(BF16), pb)
            else:
                z = _dot(jnp.concatenate([t_inv[h], a_pow[h]], axis=0).astype(BF16), pb)
                t_inv[h] = t_inv[h] + z[:L]
                a_pow[h] = z[L:]

    u = [_dot(t_inv[h].astype(BF16), rhs[h]).astype(BF16) for h in heads]
    for h in heads:
        y_ref[:, hs[h]] = p_s[h][L:] + _dot(a_rb[h], u[h]) + _dot(a_rk[h], vb[:, hs[h]])
    for h in heads:
        uv = jnp.concatenate([u[h], vb[:, hs[h]]], axis=0)
        bk = jnp.concatenate([bw[:, hs[h]], kw[:, hs[h]]], axis=0)
        s_sc[h] = s_old[h] * ewl[:, hs[h]] + _dot_tn(uv, bk)

    @pl.when(c == n_chunks - 1)
    def _():
        sout_ref[...] = s_sc[...]


def _wkv(r, lw, k, v, a, b, s0, chunk):
    m, d = r.shape
    n_seq, n_heads = s0.shape[0], s0.shape[1]
    n_chunks = m // (n_seq * chunk)
    row = pl.BlockSpec((chunk, d), lambda s, c: (s * n_chunks + c, 0))
    st = pl.BlockSpec((None, n_heads, RWKV_HEAD, RWKV_HEAD), lambda s, c: (s, 0, 0, 0))
    return pl.pallas_call(
        functools.partial(_wkv_body, n_heads=n_heads),
        grid=(n_seq, n_chunks),
        in_specs=[row] * 6 + [st],
        out_specs=[row, st],
        out_shape=[jax.ShapeDtypeStruct((m, d), F32), jax.ShapeDtypeStruct(s0.shape, F32)],
        scratch_shapes=[pltpu.VMEM((n_heads, RWKV_HEAD, RWKV_HEAD), F32)],
        compiler_params=_params(2),
        name="wkv7_chunked",
    )(r, lw, k, v, a, b, s0)


def _rwkv_out_body(x_ref, y_ref, bonus_ref, g_ref, lg_ref, lb_ref, pd_ref, pu_ref, w_ref, o_ref):
    pd = pd_ref[...]
    pu = pu_ref[...]
    y = y_ref[...]
    inv_n = 1.0 / RWKV_HEAD
    yc = y - _seg_sum(y, pd, pu) * inv_n
    var = _seg_sum(yc * yc, pd, pu) * inv_n
    yn = yc * lax.rsqrt(var + LNX_EPS) * lg_ref[...] + lb_ref[...]
    o_ref[...] = x_ref[...] + _dot(((yn + bonus_ref[...]) * g_ref[...]).astype(BF16), w_ref[...])


def _rwkv_out(x, y, bonus, g, w, tm):
    m, d = x.shape
    row = _row_spec(tm, d)
    vec = _const_spec((1, d))
    return pl.pallas_call(
        _rwkv_out_body,
        grid=(m // tm,),
        in_specs=[row, row, row, row, vec, vec, _const_spec((d, LANES)), _const_spec((LANES, d)),
                  _const_spec((d, d))],
        out_specs=row,
        out_shape=jax.ShapeDtypeStruct((m, d), F32),
        compiler_params=_params(1),
        name="rwkv_out_proj",
    )(x, y, bonus, g, w["lnx_g"], w["lnx_b"], w["pd"], w["pu"], w["wo"])


def _ffn_body(*refs, stride, tiles_per_seq, n_chunks, has_buf, final):
    x_ref, g_ref, wg_ref, wu_ref, cw_ref, cb_ref, wd_ref = refs[:7]
    pos = 7
    buf_ref = gf_ref = None
    if has_buf:
        buf_ref = refs[pos]
        pos += 1
    if final:
        gf_ref = refs[pos]
        pos += 1
    o_ref, cv_ref, carry_sc, acc_sc = refs[pos:pos + 4]
    i = pl.program_id(0)
    x = x_ref[...]
    tm = x.shape[0]
    fc = wg_ref.shape[2]
    h = _rms(x, g_ref[...]).astype(BF16)
    acc_sc[...] = jnp.zeros(acc_sc.shape, F32)
    if not has_buf:
        @pl.when(i % tiles_per_seq == 0)
        def _():
            carry_sc[...] = jnp.zeros(carry_sc.shape, F32)
        row = lax.broadcasted_iota(jnp.int32, (tm, fc), 0)

    def chunk(c, carry):
        u = _dot(h, wg_ref[c])
        up = _dot(h, wu_ref[c])
        cw = cw_ref[c]
        if has_buf:
            full = jnp.concatenate([buf_ref[c], u], axis=0)
            u2 = full[0:tm]
            u1 = full[stride:stride + tm]
            cv_ref[c] = full[tm:tm + 2 * stride]
        else:
            prev = carry_sc[c]
            u1 = jnp.where(row == 0, prev[1:2], pltpu.roll(u, 1, 0))
            u2 = jnp.where(row == 0, prev[0:1], jnp.where(row == 1, prev[1:2], pltpu.roll(u, 2, 0)))
            carry_sc[c] = u[tm - 2:tm]
            cv_ref[c] = u[tm - 2:tm]
        cv = cb_ref[c] + u2 * cw[0:1] + u1 * cw[1:2] + u * cw[2:3]
        act = (cv * jax.nn.sigmoid(cv) * up).astype(BF16)
        acc_sc[...] += _dot(act, wd_ref[c])
        return carry

    for c in range(n_chunks):
        chunk(c, 0)
    out = x + acc_sc[...]
    if final:
        out = _rms(out, gf_ref[...])
    o_ref[...] = out


def _ffn(x, norm_g, w, tm, stride, tiles_per_seq, buf0, final_g):
    m, d = x.shape
    n_chunks, _, fc = w["wg"].shape
    has_buf = buf0 is not None
    final = final_g is not None
    n_seq_blocks = (m // tm) // tiles_per_seq
    keep = 2 * stride
    seq_spec = pl.BlockSpec((None, n_chunks, keep, fc), lambda i: (i // tiles_per_seq, 0, 0, 0))
    ins = [x, norm_g.reshape(1, d), w["wg"], w["wu"], w["cw"], w["cb"], w["wd"]]
    in_specs = [_row_spec(tm, d), _const_spec((1, d)), _const_spec((n_chunks, d, fc)),
                _const_spec((n_chunks, d, fc)), _const_spec((n_chunks, 3, fc)), _const_spec((n_chunks, 1, fc)),
                _const_spec((n_chunks, fc, d))]
    if has_buf:
        ins.append(buf0)
        in_specs.append(seq_spec)
    if final:
        ins.append(final_g.reshape(1, d))
        in_specs.append(_const_spec((1, d)))
    return pl.pallas_call(
        functools.partial(_ffn_body, stride=stride, tiles_per_seq=tiles_per_seq, n_chunks=n_chunks,
                          has_buf=has_buf, final=final),
        grid=(m // tm,),
        in_specs=in_specs,
        out_specs=[_row_spec(tm, d), seq_spec],
        out_shape=[jax.ShapeDtypeStruct((m, d), F32),
                   jax.ShapeDtypeStruct((n_seq_blocks, n_chunks, keep, fc), F32)],
        scratch_shapes=[pltpu.VMEM((n_chunks, 2, fc), F32), pltpu.VMEM((tm, d), F32)],
        compiler_params=_params(1),
        name="conv_ffn",
    )(*ins)


def _pad_cols(w, n):
    return jnp.pad(w, ((0, 0), (0, n - w.shape[1])))


def _pad_rows(w, n):
    return jnp.pad(w, ((0, n - w.shape[0]), (0, 0)))


def _round_up(n, mult):
    return -(-n // mult) * mult


def _head_maps(d, head):
    onehot = (jnp.arange(d)[:, None] // head == jnp.arange(LANES)[None, :]).astype(BF16)
    return onehot, onehot.T


def _row_tile(rows):
    tm = min(ROW_TILE, rows)
    assert rows % tm == 0
    return tm


def kernel(x_prompt, x_sample, cache_k, cache_v, page_table, state_shift, state_wkv, state_ffn_conv, attn_norm_g, w_qkv, lambda_q1, lambda_k1, lambda_q2, lambda_k2, subln_g, w_o_attn, rwkv_norm_g, rwkv_mix, w_rkv, w_decay0, w_decay1, w_decay2, w_iclr0, w_iclr1, w_iclr2, rwkv_g1, rwkv_g2, k_k, k_a, r_k, lnx_g, lnx_b, w_o_rwkv, ffn_norm_g, ffn_w_gate, ffn_w_up, ffn_conv_w, ffn_conv_b, ffn_w_down, final_norm_g):
    B, T, D = x_prompt.shape
    DB, TS, _ = x_sample.shape
    depth = ffn_norm_g.shape[0]
    F = ffn_w_gate.shape[2]
    n_attn_heads = D // (2 * ATTN_HEAD_DIM)
    n_rwkv_heads = D // RWKV_HEAD
    hd = 2 * ATTN_HEAD_DIM
    tm_p = _row_tile(T)
    tps_p = T // tm_p
    ms = TS * DB
    fc = FFN_CHUNK
    n_chunks = F // fc
    assert F % fc == 0 and T % WKV_CHUNK == 0 and TS <= WKV_CHUNK and DB % 8 == 0

    xp = x_prompt.reshape(B * T, D)
    xs = x_sample.transpose(1, 0, 2).reshape(ms, D)
    pd, pu = _head_maps(D, RWKV_HEAD)

    kp_l, vp_l, ks_l, vs_l = [], [], [], []
    shp_l, shs_l, wkp_l, wks_l = [], [], [], []
    cvp_l, cvs_l = [], []
    for i in range(depth):
        j = i // 2
        if i % 2 == 0:
            lam_init = 0.8 - 0.6 * math.exp(-0.3 * i)
            lams = [v[j].reshape(1, -1) for v in (lambda_q1, lambda_k1, lambda_q2, lambda_k2)]
            wqkv = w_qkv[j].astype(BF16)
            wo = w_o_attn[j].astype(BF16)
            q0b, q1b, kb, vb, k32, v32 = _qkv(xp, attn_norm_g[j], wqkv, tm_p)
            o = _flash(q0b, q1b, kb, vb, lams, subln_g[j], B, T, lam_init)
            xp = _proj_res(xp, o, wo, tm_p)
            kp_l.append(k32.reshape(B, T, n_attn_heads, hd))
            vp_l.append(v32.reshape(B, T, n_attn_heads, hd))
            q0b, q1b, _, _, k32, v32 = _qkv(xs, attn_norm_g[j], wqkv, ms)
            qb = q0b + q1b
            k_new = k32.reshape(TS, DB, n_attn_heads, hd).transpose(1, 0, 2, 3)
            v_new = v32.reshape(TS, DB, n_attn_heads, hd).transpose(1, 0, 2, 3)
            ks_l.append(k_new)
            vs_l.append(v_new)
            q5 = qb.reshape(TS, DB, n_attn_heads, 2, ATTN_HEAD_DIM).transpose(1, 2, 3, 0, 4)
            q_all = (q5[:, :, :, :, None, :] * jnp.eye(2, dtype=BF16)[None, None, :, None, :, None])
            q_all = q_all.reshape(DB, n_attn_heads * 2 * TS, hd)
            new_pad = _round_up(TS, 16)
            pad = ((0, 0), (0, new_pad - TS), (0, 0), (0, 0))
            o = _decode(q_all, cache_k, cache_v, j, page_table, jnp.pad(k_new, pad), jnp.pad(v_new, pad),
                        lams, subln_g[j], lam_init)
            o = o.reshape(DB, n_attn_heads, 2, TS, hd)[:, :, 0].transpose(2, 0, 1, 3).reshape(ms, D)
            xs = _proj_res(xs, o.astype(BF16), wo, ms)
        else:
            lora = _round_up(w_decay1.shape[2], LANES)
            gate = _round_up(rwkv_g1.shape[2], LANES)
            w = dict(
                mix=rwkv_mix[j], rkv=w_rkv[j].astype(BF16),
                w0=w_decay0[j].reshape(1, D), d1=_pad_cols(w_decay1[j], lora).astype(BF16),
                d2=_pad_rows(w_decay2[j], lora).astype(BF16),
                a0=w_iclr0[j].reshape(1, D), i1=_pad_cols(w_iclr1[j], lora).astype(BF16),
                i2=_pad_rows(w_iclr2[j], lora).astype(BF16),
                g1=_pad_cols(rwkv_g1[j], gate).astype(BF16), g2=_pad_rows(rwkv_g2[j], gate).astype(BF16),
                k_k=k_k[j].reshape(1, D), k_a=k_a[j].reshape(1, D), r_k=r_k[j].reshape(1, D),
                lnx_g=lnx_g[j].reshape(1, D), lnx_b=lnx_b[j].reshape(1, D),
                wo=w_o_rwkv[j].astype(BF16), pd=pd, pu=pu)
            outs = _rwkv_proj(xp, rwkv_norm_g[j], jnp.zeros((B, 1, D), F32), w, tm_p, 1, tps_p)
            r, lw, k, v, nkk, kka, g, bonus, sh = outs
            y, s_fin = _wkv(r, lw, k, v, nkk, kka, jnp.zeros((B, n_rwkv_heads, RWKV_HEAD, RWKV_HEAD), F32),
                            WKV_CHUNK)
            xp = _rwkv_out(xp, y, bonus, g, w, tm_p)
            shp_l.append(sh.reshape(B, D))
            wkp_l.append(s_fin)
            outs = _rwkv_proj(xs, rwkv_norm_g[j], state_shift[j][None], w, ms, DB, 1)
            r, lw, k, v, nkk, kka, g, bonus, sh = outs

            def to_seq(a):
                a = a.reshape(TS, DB, D).transpose(1, 0, 2)
                return jnp.pad(a, ((0, 0), (0, WKV_CHUNK - TS), (0, 0))).reshape(DB * WKV_CHUNK, D)

            y, s_fin = _wkv(*(to_seq(a) for a in (r, lw, k, v, nkk, kka)), state_wkv[j], WKV_CHUNK)
            y = y.reshape(DB, WKV_CHUNK, D)[:, :TS].transpose(1, 0, 2).reshape(ms, D)
            xs = _rwkv_out(xs, y, bonus, g, w, ms)
            shs_l.append(sh.reshape(DB, D))
            wks_l.append(s_fin)
        w = dict(
            wg=ffn_w_gate[i].astype(BF16).reshape(D, n_chunks, fc).transpose(1, 0, 2),
            wu=ffn_w_up[i].astype(BF16).reshape(D, n_chunks, fc).transpose(1, 0, 2),
            cw=ffn_conv_w[i].reshape(3, n_chunks, fc).transpose(1, 0, 2),
            cb=ffn_conv_b[i].reshape(n_chunks, 1, fc),
            wd=ffn_w_down[i].astype(BF16).reshape(n_chunks, fc, D))
        final_g = final_norm_g if i == depth - 1 else None
        tm_f = min(FFN_ROW_TILE, T)
        xp, cv = _ffn(xp, ffn_norm_g[i], w, tm_f, 1, T // tm_f, None, final_g)
        cvp_l.append(cv.transpose(0, 2, 1, 3).reshape(B, 2, F))
        buf0 = state_ffn_conv[i].reshape(DB, 2, n_chunks, fc).transpose(2, 1, 0, 3).reshape(1, n_chunks, 2 * DB, fc)
        xs, cv = _ffn(xs, ffn_norm_g[i], w, ms, DB, 1, buf0, final_g)
        cvs_l.append(cv.reshape(n_chunks, 2, DB, fc).transpose(2, 1, 0, 3).reshape(DB, 2, F))

    y_prompt = xp.reshape(B, T, D)
    y_sample = xs.reshape(TS, DB, D).transpose(1, 0, 2)
    return (y_prompt, y_sample,
            jnp.stack(kp_l), jnp.stack(vp_l), jnp.stack(ks_l), jnp.stack(vs_l),
            jnp.stack(shp_l), jnp.stack(shs_l), jnp.stack(wkp_l), jnp.stack(wks_l),
            jnp.stack(cvp_l), jnp.stack(cvs_l))
```

```python
import functools
import math

import jax
import jax.numpy as jnp
from jax import lax
from jax.experimental import pallas as pl
from jax.experimental.pallas import tpu as pltpu

F32 = jnp.float32
BF16 = jnp.bfloat16

RMS_EPS = 1e-6
SUBLN_EPS = 1e-5
LNX_EPS = 64e-5
ATTN_HEAD_DIM = 64
RWKV_HEAD = 64
LANES = 128
NEG_BIG = -1e30
LOG2E = 1.4426950408889634
VMEM_LIMIT_BYTES = 56 * 1024 * 1024

ROW_TILE = 512
FFN_ROW_TILE = 1024
FLASH_TILE = 256
FFN_CHUNK = 256
DECODE_PAGES_PER_STEP = 16
WKV_CHUNK = 64
WKV_CHUNKS_PER_STEP = 4

NT_DIMS = (((1,), (1,)), ((), ()))
TN_DIMS = (((0,), (0,)), ((), ()))


def _params(n_axes):
    return pltpu.CompilerParams(dimension_semantics=("arbitrary",) * n_axes,
                                vmem_limit_bytes=VMEM_LIMIT_BYTES)


def _dot(a, b):
    return jnp.dot(a, b, preferred_element_type=F32)


def _dot_nt(a, b):
    return lax.dot_general(a, b, NT_DIMS, preferred_element_type=F32)


def _dot_tn(a, b):
    return lax.dot_general(a, b, TN_DIMS, preferred_element_type=F32)


def _rms(x, g, eps=RMS_EPS):
    return x * lax.rsqrt(jnp.mean(x * x, axis=-1, keepdims=True) + eps) * g


def _const_spec(shape):
    nd = len(shape)
    return pl.BlockSpec(shape, lambda *_: (0,) * nd, pipeline_mode=pl.Buffered(1))


def _row_spec(tm, d):
    return pl.BlockSpec((tm, d), lambda i: (i, 0))


def _tile_lanes(x, width):
    if width <= LANES:
        return x[:, :width]
    return jnp.concatenate([x] * (width // LANES), axis=1)


def _lambda(lq1, lk1, lq2, lk2, lam_init):
    s1 = jnp.sum(lq1[...] * lk1[...], axis=-1, keepdims=True)
    s2 = jnp.sum(lq2[...] * lk2[...], axis=-1, keepdims=True)
    return jnp.exp(s1) - jnp.exp(s2) + lam_init


def _qkv_body(x_ref, g_ref, w_ref, q0_ref, q1_ref, kb_ref, vb_ref, k_ref, v_ref):
    d = x_ref.shape[1]
    h = _rms(x_ref[...], g_ref[...]).astype(BF16)
    q = (_dot(h, w_ref[:, 0:d]) * (ATTN_HEAD_DIM ** -0.5 * LOG2E)).astype(BF16)
    first_map = (lax.broadcasted_iota(jnp.int32, q.shape, 1) // ATTN_HEAD_DIM) % 2 == 0
    zero = jnp.zeros_like(q)
    q0_ref[...] = jnp.where(first_map, q, zero)
    q1_ref[...] = jnp.where(first_map, zero, q)
    k = _dot(h, w_ref[:, d:2 * d])
    k_ref[...] = k
    kb_ref[...] = k.astype(BF16)
    v = _dot(h, w_ref[:, 2 * d:3 * d])
    v_ref[...] = v
    vb_ref[...] = v.astype(BF16)


def _qkv(x, g, w_bf, tm):
    m, d = x.shape
    bf = jax.ShapeDtypeStruct((m, d), BF16)
    f = jax.ShapeDtypeStruct((m, d), F32)
    return pl.pallas_call(
        _qkv_body,
        grid=(m // tm,),
        in_specs=[_row_spec(tm, d), _const_spec((1, d)), _const_spec((d, 3 * d))],
        out_specs=[_row_spec(tm, d)] * 6,
        out_shape=[bf, bf, bf, bf, f, f],
        compiler_params=_params(1),
        name="qkv_proj",
    )(x, g.reshape(1, d), w_bf)


def _flash_body(q0_ref, q1_ref, k_ref, v_ref, lq1, lk1, lq2, lk2, sg_ref, o_ref, m_sc, l_sc, acc_sc,
                *, tile, n_heads, lam_init):
    qi = pl.program_id(1)
    q_start = qi * tile
    hd = 2 * ATTN_HEAD_DIM
    q_refs = (q0_ref, q1_ref)
    m_sc[...] = jnp.full(m_sc.shape, NEG_BIG, F32)
    l_sc[...] = jnp.zeros(l_sc.shape, F32)
    acc_sc[...] = jnp.zeros(acc_sc.shape, F32)
    col_row = lax.broadcasted_iota(jnp.int32, (1, tile), 1)

    def kv_step(j, on_diagonal):
        k0 = pl.multiple_of(j * tile, tile)
        rel = (k0 - q_start + col_row).astype(F32)
        if on_diagonal:
            keep = (lax.broadcasted_iota(jnp.int32, (tile, tile), 1)
                    <= lax.broadcasted_iota(jnp.int32, (tile, tile), 0))
        for h in range(n_heads):
            hs = slice(h * hd, (h + 1) * hd)
            bias = (LOG2E * 2.0 ** (-8.0 * (h + 1) / n_heads)) * rel
            kj = k_ref[pl.ds(k0, tile), hs]
            vj = v_ref[pl.ds(k0, tile), hs]
            for c in range(2):
                idx = 2 * h + c
                s = _dot_nt(q_refs[c][:, hs], kj) + bias
                if on_diagonal:
                    s = jnp.where(keep, s, NEG_BIG)
                m_prev = m_sc[idx]
                m_new = jnp.maximum(m_prev, jnp.max(s, axis=-1, keepdims=True))
                p = jnp.exp2(s - _tile_lanes(m_new, tile))
                alpha = jnp.exp2(m_prev - m_new)
                l_sc[idx] = alpha * l_sc[idx] + jnp.sum(p, axis=-1, keepdims=True)
                acc_sc[idx] = alpha * acc_sc[idx] + _dot(p.astype(BF16), vj)
                m_sc[idx] = m_new

    def off_diagonal(j, carry):
        kv_step(j, False)
        return carry

    lax.fori_loop(0, qi, off_diagonal, 0)
    kv_step(qi, True)

    lam = _lambda(lq1, lk1, lq2, lk2, lam_init)
    sg = sg_ref[...]
    for h in range(n_heads):
        o = acc_sc[2 * h] / l_sc[2 * h] - lam * (acc_sc[2 * h + 1] / l_sc[2 * h + 1])
        o = o * lax.rsqrt(jnp.mean(o * o, axis=-1, keepdims=True) + SUBLN_EPS) * sg * (1.0 - lam_init)
        o_ref[:, h * hd:(h + 1) * hd] = o.astype(BF16)


def _flash(q0b, q1b, kb, vb, lams, subln_g, n_seq, seq, lam_init):
    m, d = q0b.shape
    tile = min(FLASH_TILE, seq)
    nq = seq // tile
    hd = 2 * ATTN_HEAD_DIM
    n_heads = d // hd
    q_spec = pl.BlockSpec((tile, d), lambda b, i: (b * nq + i, 0))
    kv_spec = pl.BlockSpec((seq, d), lambda b, i: (b, 0))
    return pl.pallas_call(
        functools.partial(_flash_body, tile=tile, n_heads=n_heads, lam_init=lam_init),
        grid=(n_seq, nq),
        in_specs=[q_spec, q_spec, kv_spec, kv_spec] + [_const_spec((1, ATTN_HEAD_DIM))] * 4
                 + [_const_spec((1, hd))],
        out_specs=q_spec,
        out_shape=jax.ShapeDtypeStruct((m, d), BF16),
        scratch_shapes=[pltpu.VMEM((2 * n_heads, tile, LANES), F32), pltpu.VMEM((2 * n_heads, tile, LANES), F32),
                        pltpu.VMEM((2 * n_heads, tile, hd), F32)],
        compiler_params=_params(2),
        name="prompt_attention",
    )(q0b, q1b, kb, vb, *lams, subln_g.reshape(1, -1))


def _decode_body(*refs, past_len, page, n_heads, dec_seq, pages_per_step, lam_init):
    npp = pages_per_step
    q_ref = refs[1]
    kc_refs = refs[2:2 + npp]
    vc_refs = refs[2 + npp:2 + 2 * npp]
    (kn_ref, vn_ref, slope_ref, madd_ref, tok_ref, slope_n_ref, madd_n_ref, tok_n_ref,
     lq1, lk1, lq2, lk2, sg_ref, o_ref, m_sc, l_sc, acc_sc) = refs[2 + 2 * npp:]
    p = pl.program_id(1)
    n_steps = pl.num_programs(1)
    hd = 2 * ATTN_HEAD_DIM

    @pl.when(p == 0)
    def _():
        m_sc[...] = jnp.full(m_sc.shape, NEG_BIG, F32)
        l_sc[...] = jnp.zeros(l_sc.shape, F32)
        acc_sc[...] = jnp.zeros(acc_sc.shape, F32)

    q = q_ref[...]

    def update(k_list, v_list, base_bias, row_shift):
        s = [_dot_nt(q, kf.astype(BF16)) + base_bias for kf in k_list]
        m_prev = m_sc[...]
        m_new = m_prev
        for si, ri in zip(s, row_shift):
            m_new = jnp.maximum(m_new, jnp.max(si, axis=-1, keepdims=True) + ri)
        alpha = jnp.exp2(m_prev - m_new)
        l_new = alpha * l_sc[...]
        acc = alpha * acc_sc[...]
        for si, ri, vf in zip(s, row_shift, v_list):
            pm = jnp.exp2(si - _tile_lanes(m_new - ri, si.shape[1]))
            l_new = l_new + jnp.sum(pm, axis=-1, keepdims=True)
            acc = acc + _dot(pm.astype(BF16), vf.astype(BF16))
        l_sc[...] = l_new
        acc_sc[...] = acc
        m_sc[...] = m_new

    rows = page * n_heads
    slope_col = slope_ref[:, 0:LANES]
    rel = tok_ref[...] + (p * (npp * page) - past_len).astype(F32)
    update([r[...].reshape(rows, hd) for r in kc_refs], [r[...].reshape(rows, hd) for r in vc_refs],
           slope_ref[...] * rel + madd_ref[...], [slope_col * float(i * page) for i in range(npp)])

    @pl.when(p == n_steps - 1)
    def _():
        rows_n = kn_ref.shape[0] * n_heads
        update([kn_ref[...].reshape(rows_n, hd)], [vn_ref[...].reshape(rows_n, hd)],
               slope_n_ref[...] * tok_n_ref[...] + madd_n_ref[...], [0.0])
        lam = _lambda(lq1, lk1, lq2, lk2, lam_init)
        an = acc_sc[...] / l_sc[...]
        n_rows = an.shape[0]
        o = an - lam * pltpu.roll(an, n_rows - dec_seq, 0)
        o = o * lax.rsqrt(jnp.mean(o * o, axis=-1, keepdims=True) + SUBLN_EPS) * sg_ref[...] * (1.0 - lam_init)
        o_ref[...] = o


def _decode(q_all, cache_k, cache_v, layer, page_table, k_new, v_new, lams, subln_g, lam_init):
    db, n_rows, hd = q_all.shape
    _, _, page, n_heads, _ = cache_k.shape
    n_pages = page_table.shape[1]
    past_len = n_pages * page
    dec_seq = n_rows // (2 * n_heads)
    new_pad = k_new.shape[1]
    npp = max(c for c in (DECODE_PAGES_PER_STEP, 8, 4, 2, 1) if n_pages % c == 0)
    slopes = [LOG2E * 2.0 ** (-8.0 * (h + 1) / n_heads) for h in range(n_heads)]

    def consts(n_tok, causal):
        r = jnp.arange(n_rows)
        c = jnp.arange(n_tok * n_heads)
        r_head, r_tok = r // (2 * dec_seq), r % dec_seq
        c_tok, c_head = c // n_heads, c % n_heads
        ok = r_head[:, None] == c_head[None, :]
        if causal:
            ok = ok & (c_tok[None, :] <= r_tok[:, None]) & (c_tok[None, :] < dec_seq)
        slope = jnp.asarray(slopes, F32)[r_head][:, None] * jnp.ones((1, c.shape[0]), F32)
        return slope, jnp.where(ok, 0.0, NEG_BIG).astype(F32), c_tok.astype(F32)[None, :]

    slope_p, madd_p, tok_p = consts(page, False)
    slope_n, madd_n, tok_n = consts(new_pad, True)
    cols_p, cols_n = page * n_heads, new_pad * n_heads
    def cache_spec(i):
        return pl.BlockSpec((None, None, page, n_heads, hd),
                            lambda b, p, pt: (layer, pt[b, p * npp + i], 0, 0, 0))

    new_spec = pl.BlockSpec((None, new_pad, n_heads, hd), lambda b, p, pt: (b, 0, 0, 0))

    def cst(shape):
        nd = len(shape)
        return pl.BlockSpec(shape, lambda b, p, pt: (0,) * nd)

    grid_spec = pltpu.PrefetchScalarGridSpec(
        num_scalar_prefetch=1,
        grid=(db, n_pages // npp),
        in_specs=[pl.BlockSpec((None, n_rows, hd), lambda b, p, pt: (b, 0, 0))]
                 + [cache_spec(i) for i in range(npp)] * 2
                 + [new_spec, new_spec,
                    cst((n_rows, cols_p)), cst((n_rows, cols_p)), cst((1, cols_p)),
                    cst((n_rows, cols_n)), cst((n_rows, cols_n)), cst((1, cols_n))]
                 + [cst((1, ATTN_HEAD_DIM))] * 4 + [cst((1, hd))],
        out_specs=pl.BlockSpec((None, n_rows, hd), lambda b, p, pt: (b, 0, 0)),
        scratch_shapes=[pltpu.VMEM((n_rows, LANES), F32), pltpu.VMEM((n_rows, LANES), F32),
                        pltpu.VMEM((n_rows, hd), F32)],
    )
    return pl.pallas_call(
        functools.partial(_decode_body, past_len=past_len, page=page, n_heads=n_heads,
                          dec_seq=dec_seq, pages_per_step=npp, lam_init=lam_init),
        grid_spec=grid_spec,
        out_shape=jax.ShapeDtypeStruct((db, n_rows, hd), F32),
        compiler_params=_params(2),
        name="decode_attention",
    )(page_table, q_all, *([cache_k] * npp), *([cache_v] * npp), k_new, v_new, slope_p, madd_p, tok_p,
      slope_n, madd_n, tok_n, *lams, subln_g.reshape(1, -1))


def _proj_res_body(x_ref, a_ref, w_ref, o_ref):
    o_ref[...] = x_ref[...] + _dot(a_ref[...], w_ref[...])


def _proj_res(x, a_bf, w_bf, tm):
    m, d = x.shape
    return pl.pallas_call(
        _proj_res_body,
        grid=(m // tm,),
        in_specs=[_row_spec(tm, d), _row_spec(tm, d), _const_spec((d, d))],
        out_specs=_row_spec(tm, d),
        out_shape=jax.ShapeDtypeStruct((m, d), F32),
        compiler_params=_params(1),
        name="attn_out_proj",
    )(x, a_bf, w_bf)


def _shifted(h, prev, stride):
    tm = h.shape[0]
    if stride % 8 == 0:
        return jnp.concatenate([prev, h[:tm - stride]], axis=0)
    assert stride == 1
    row = lax.broadcasted_iota(jnp.int32, h.shape, 0)
    return jnp.where(row == 0, prev, pltpu.roll(h, 1, 0))


def _rwkv_proj_body(x_ref, g_ref, sh0_ref, mix_ref, wrkv_ref, w0_ref, d1_ref, d2_ref, a0_ref, i1_ref, i2_ref,
                    g1_ref, g2_ref, r_out, lw_out, k_out, v_out, a_out, g_out, sh_out,
                    carry_sc, *, stride, tiles_per_seq):
    i = pl.program_id(0)
    h = _rms(x_ref[...], g_ref[...])
    tm = h.shape[0]

    @pl.when(i % tiles_per_seq == 0)
    def _():
        carry_sc[...] = sh0_ref[...]

    hp = _shifted(h, carry_sc[...], stride)
    carry_sc[...] = h[tm - stride:]
    sh_out[...] = h[tm - stride:]
    xx = hp - h
    mix = mix_ref[...]

    def xm(s):
        return (h + xx * mix[s:s + 1]).astype(BF16)

    r_out[...] = _dot(xm(0), wrkv_ref[0])
    k_out[...] = _dot(xm(1), wrkv_ref[1])
    v_out[...] = _dot(xm(2), wrkv_ref[2])
    lora_w = _dot(jnp.tanh(_dot(xm(3), d1_ref[...])).astype(BF16), d2_ref[...])
    z = -(w0_ref[...] + lora_w)
    softplus = jnp.maximum(z, 0.0) + jnp.log(1.0 + jnp.exp(-jnp.abs(z)))
    lw_out[...] = -jnp.exp(-softplus - 0.5)
    a_out[...] = jax.nn.sigmoid(a0_ref[...] + _dot(_dot(xm(4), i1_ref[...]).astype(BF16), i2_ref[...]))
    g_out[...] = _dot(jax.nn.sigmoid(_dot(xm(5), g1_ref[...])).astype(BF16), g2_ref[...])


def _rwkv_proj(x, norm_g, shift0, w, tm, stride, tiles_per_seq):
    m, d = x.shape
    n_seq_blocks = shift0.shape[0]
    row = _row_spec(tm, d)
    f = jax.ShapeDtypeStruct((m, d), F32)
    vec = _const_spec((1, d))
    seq_spec = pl.BlockSpec((None, stride, d), lambda i: (i // tiles_per_seq, 0, 0))
    lora = w["d1"].shape[1]
    gate = w["g1"].shape[1]
    return pl.pallas_call(
        functools.partial(_rwkv_proj_body, stride=stride, tiles_per_seq=tiles_per_seq),
        grid=(m // tm,),
        in_specs=[row, vec, seq_spec, _const_spec((6, d)), _const_spec((3, d, d)),
                  vec, _const_spec((d, lora)), _const_spec((lora, d)),
                  vec, _const_spec((d, lora)), _const_spec((lora, d)),
                  _const_spec((d, gate)), _const_spec((gate, d))],
        out_specs=[row] * 6 + [seq_spec],
        out_shape=[f] * 6 + [jax.ShapeDtypeStruct((n_seq_blocks, stride, d), F32)],
        scratch_shapes=[pltpu.VMEM((stride, d), F32)],
        compiler_params=_params(1),
        name="rwkv_proj",
    )(x, norm_g.reshape(1, d), shift0, w["mix"], w["rkv"], w["w0"], w["d1"], w["d2"], w["a0"], w["i1"], w["i2"],
      w["g1"], w["g2"])


def _head_sums(x):
    rows, d = x.shape
    assert LANES == 2 * RWKV_HEAD and d % LANES == 0
    low = lax.broadcasted_iota(jnp.int32, (rows, LANES), 1) < RWKV_HEAD
    out = []
    for c in range(d // LANES):
        xc = x[:, c * LANES:(c + 1) * LANES]
        s_lo = jnp.sum(jnp.where(low, xc, 0.0), axis=-1, keepdims=True)
        s_hi = jnp.sum(jnp.where(low, 0.0, xc), axis=-1, keepdims=True)
        out.append(jnp.where(low, s_lo, s_hi))
    return jnp.concatenate(out, axis=1)


def _wkv_body(r_ref, lw_ref, k_ref, v_ref, ia_ref, g_ref, kk_ref, ka_ref, rk_ref, lg_ref, lb_ref, s0_ref,
              o_ref, sout_ref, s_sc, y_sc, *, n_heads, chunk):
    c = pl.program_id(1)

    @pl.when(c == 0)
    def _():
        s_sc[...] = s0_ref[...]

    state = [s_sc[h] for h in range(n_heads)]
    for sub in range(r_ref.shape[0] // chunk):
        rows = slice(sub * chunk, (sub + 1) * chunk)
        state = _wkv_chunk(rows, r_ref, lw_ref, k_ref, v_ref, ia_ref, g_ref, kk_ref, ka_ref, rk_ref, lg_ref,
                           lb_ref, o_ref, y_sc, state)
    for h in range(n_heads):
        s_sc[h] = state[h]

    @pl.when(c == pl.num_programs(1) - 1)
    def _():
        sout_ref[...] = s_sc[...]


def _wkv_chunk(rows, r_ref, lw_ref, k_ref, v_ref, ia_ref, g_ref, kk_ref, ka_ref, rk_ref, lg_ref, lb_ref,
               o_ref, y_sc, s_old):
    L = rows.stop - rows.start
    N = RWKV_HEAD
    n_heads = len(s_old)

    iclr = ia_ref[rows, :]
    k_raw = k_ref[rows, :]
    kk = k_raw * kk_ref[...]
    kk = kk / jnp.maximum(jnp.sqrt(_head_sums(kk * kk)), 1e-12)
    a_in = -kk
    b_in = kk * iclr
    k_in = k_raw * (1.0 + (iclr - 1.0) * ka_ref[...])
    r_in = r_ref[rows, :]
    v_in = v_ref[rows, :]
    bonus = _head_sums(r_in * k_in * rk_ref[...]) * v_in

    row = lax.broadcasted_iota(jnp.int32, (L, L), 0)
    col = lax.broadcasted_iota(jnp.int32, (L, L), 1)
    incl = row >= col
    strict = row > col
    tri = incl.astype(F32).astype(BF16)
    eye = (row == col).astype(F32)

    lw = lw_ref[rows, :]
    hi = lw.astype(BF16)
    r1 = lw - hi.astype(F32)
    mid = r1.astype(BF16)
    lo = (r1 - mid.astype(F32)).astype(BF16)
    cum = _dot(tri, hi) + _dot(tri, mid) + _dot(tri, lo)
    wl = cum[L - 1:L]
    e_in = jnp.exp(cum)
    e_out = jnp.exp(-cum)
    e_tail = jnp.exp(wl - cum)
    at = (a_in * jnp.exp(cum - lw)).astype(BF16)
    rt = (r_in * e_in).astype(BF16)
    bt = (b_in * e_out).astype(BF16)
    kt = (k_in * e_out).astype(BF16)
    bw = (b_in * e_tail).astype(BF16)
    kw = (k_in * e_tail).astype(BF16)
    vb = v_in.astype(BF16)
    ewl = jnp.exp(wl)

    heads = range(n_heads)
    hs = [slice(h * N, (h + 1) * N) for h in heads]
    ar = [jnp.concatenate([at[:, s], rt[:, s]], axis=0) for s in hs]
    g_b = [_dot_nt(ar[h], bt[:, hs[h]]) for h in heads]
    g_k = [_dot_nt(ar[h], kt[:, hs[h]]) for h in heads]
    p_s = [_dot_nt(ar[h], s_old[h].astype(BF16)) for h in heads]
    a_ab = [jnp.where(strict, g[:L], 0.0) for g in g_b]
    a_rb = [jnp.where(incl, g[L:], 0.0).astype(BF16) for g in g_b]
    a_ak = [jnp.where(strict, g[:L], 0.0).astype(BF16) for g in g_k]
    a_rk = [jnp.where(incl, g[L:], 0.0).astype(BF16) for g in g_k]
    rhs = [(p_s[h][:L] + _dot(a_ak[h], vb[:, hs[h]])).astype(BF16) for h in heads]

    t_inv = [eye + a for a in a_ab]
    a_pow = []
    for a in a_ab:
        ab = a.astype(BF16)
        a_pow.append(_dot(ab, ab))
    n_steps = int(math.log2(L)) - 1
    for step in range(n_steps):
        last = step == n_steps - 1
        for h in heads:
            pb = a_pow[h].astype(BF16)
            if last:
                t_inv[h] = t_inv[h] + _dot(t_inv[h].astype(BF16), pb)
            else:
                z = _dot(jnp.concatenate([t_inv[h], a_pow[h]], axis=0).astype(BF16), pb)
                t_inv[h] = t_inv[h] + z[:L]
                a_pow[h] = z[L:]

    u = [_dot(t_inv[h].astype(BF16), rhs[h]).astype(BF16) for h in heads]
    for h in heads:
        y_sc[rows, hs[h]] = p_s[h][L:] + _dot(a_rb[h], u[h]) + _dot(a_rk[h], vb[:, hs[h]])
    s_new = []
    for h in heads:
        uv = jnp.concatenate([u[h], vb[:, hs[h]]], axis=0)
        bk = jnp.concatenate([bw[:, hs[h]], kw[:, hs[h]]], axis=0)
        s_new.append(s_old[h] * ewl[:, hs[h]] + _dot_tn(uv, bk))

    y = y_sc[rows, :]
    inv_n = 1.0 / N
    yc = y - _head_sums(y) * inv_n
    var = _head_sums(yc * yc) * inv_n
    yn = yc * lax.rsqrt(var + LNX_EPS) * lg_ref[...] + lb_ref[...]
    o_ref[rows, :] = ((yn + bonus) * g_ref[rows, :]).astype(BF16)
    return s_new


def _wkv(r, lw, k, v, iclr, g, w, s0, chunk, chunks_per_step):
    m, d = r.shape
    n_seq, n_heads = s0.shape[0], s0.shape[1]
    step_rows = chunk * chunks_per_step
    n_chunks = m // (n_seq * step_rows)
    row = pl.BlockSpec((step_rows, d), lambda s, c: (s * n_chunks + c, 0))
    st = pl.BlockSpec((None, n_heads, RWKV_HEAD, RWKV_HEAD), lambda s, c: (s, 0, 0, 0))
    vec = _const_spec((1, d))
    return pl.pallas_call(
        functools.partial(_wkv_body, n_heads=n_heads, chunk=chunk),
        grid=(n_seq, n_chunks),
        in_specs=[row] * 6 + [vec] * 5 + [st],
        out_specs=[row, st],
        out_shape=[jax.ShapeDtypeStruct((m, d), BF16), jax.ShapeDtypeStruct(s0.shape, F32)],
        scratch_shapes=[pltpu.VMEM((n_heads, RWKV_HEAD, RWKV_HEAD), F32), pltpu.VMEM((step_rows, d), F32)],
        compiler_params=_params(2),
        name="wkv7_chunked",
    )(r, lw, k, v, iclr, g, w["k_k"], w["k_a"], w["r_k"], w["lnx_g"], w["lnx_b"], s0)


def _ffn_body(*refs, stride, tiles_per_seq, n_chunks, has_buf, final):
    x_ref, g_ref, wg_ref, wu_ref, cw_ref, cb_ref, wd_ref = refs[:7]
    pos = 7
    buf_ref = gf_ref = None
    if has_buf:
        buf_ref = refs[pos]
        pos += 1
    if final:
        gf_ref = refs[pos]
        pos += 1
    o_ref, cv_ref, carry_sc, acc_sc = refs[pos:pos + 4]
    i = pl.program_id(0)
    x = x_ref[...]
    tm = x.shape[0]
    fc = wg_ref.shape[1] // n_chunks
    h = _rms(x, g_ref[...]).astype(BF16)
    acc_sc[...] = jnp.zeros(acc_sc.shape, F32)
    if not has_buf:
        @pl.when(i % tiles_per_seq == 0)
        def _():
            carry_sc[...] = jnp.zeros(carry_sc.shape, F32)
        row = lax.broadcasted_iota(jnp.int32, (tm, fc), 0)

    for c in range(n_chunks):
        cs = slice(c * fc, (c + 1) * fc)
        u = _dot(h, wg_ref[:, cs])
        up = _dot(h, wu_ref[:, cs])
        cw = cw_ref[:, cs]
        if has_buf:
            full = jnp.concatenate([buf_ref[:, cs], u], axis=0)
            u2 = full[0:tm]
            u1 = full[stride:stride + tm]
            cv_ref[:, cs] = full[tm:tm + 2 * stride]
        else:
            prev = carry_sc[:, cs]
            u1 = jnp.where(row == 0, prev[1:2], pltpu.roll(u, 1, 0))
            u2 = jnp.where(row == 0, prev[0:1], jnp.where(row == 1, prev[1:2], pltpu.roll(u, 2, 0)))
            carry_sc[:, cs] = u[tm - 2:tm]
            cv_ref[:, cs] = u[tm - 2:tm]
        cv = cb_ref[:, cs] + u2 * cw[0:1] + u1 * cw[1:2] + u * cw[2:3]
        act = (cv * jax.nn.sigmoid(cv) * up).astype(BF16)
        acc_sc[...] += _dot(act, wd_ref[cs, :])
    out = x + acc_sc[...]
    if final:
        out = _rms(out, gf_ref[...])
    o_ref[...] = out


def _ffn(x, norm_g, w, tm, stride, tiles_per_seq, buf0, final_g):
    m, d = x.shape
    f = w["wg"].shape[1]
    n_chunks = f // FFN_CHUNK
    has_buf = buf0 is not None
    final = final_g is not None
    n_seq_blocks = (m // tm) // tiles_per_seq
    keep = 2 * stride
    seq_spec = pl.BlockSpec((None, keep, f), lambda i: (i // tiles_per_seq, 0, 0))
    ins = [x, norm_g.reshape(1, d), w["wg"], w["wu"], w["cw"], w["cb"], w["wd"]]
    in_specs = [_row_spec(tm, d), _const_spec((1, d)), _const_spec((d, f)), _const_spec((d, f)),
                _const_spec((3, f)), _const_spec((1, f)), _const_spec((f, d))]
    if has_buf:
        ins.append(buf0)
        in_specs.append(seq_spec)
    if final:
        ins.append(final_g.reshape(1, d))
        in_specs.append(_const_spec((1, d)))
    return pl.pallas_call(
        functools.partial(_ffn_body, stride=stride, tiles_per_seq=tiles_per_seq, n_chunks=n_chunks,
                          has_buf=has_buf, final=final),
        grid=(m // tm,),
        in_specs=in_specs,
        out_specs=[_row_spec(tm, d), seq_spec],
        out_shape=[jax.ShapeDtypeStruct((m, d), F32),
                   jax.ShapeDtypeStruct((n_seq_blocks, keep, f), F32)],
        scratch_shapes=[pltpu.VMEM((2, f), F32), pltpu.VMEM((tm, d), F32)],
        compiler_params=_params(1),
        name="conv_ffn",
    )(*ins)


def _pad_cols(w, n):
    return jnp.pad(w, ((0, 0), (0, n - w.shape[1])))


def _pad_rows(w, n):
    return jnp.pad(w, ((0, n - w.shape[0]), (0, 0)))


def _round_up(n, mult):
    return -(-n // mult) * mult


def _row_tile(rows):
    tm = min(ROW_TILE, rows)
    assert rows % tm == 0
    return tm


def kernel(x_prompt, x_sample, cache_k, cache_v, page_table, state_shift, state_wkv, state_ffn_conv, attn_norm_g, w_qkv, lambda_q1, lambda_k1, lambda_q2, lambda_k2, subln_g, w_o_attn, rwkv_norm_g, rwkv_mix, w_rkv, w_decay0, w_decay1, w_decay2, w_iclr0, w_iclr1, w_iclr2, rwkv_g1, rwkv_g2, k_k, k_a, r_k, lnx_g, lnx_b, w_o_rwkv, ffn_norm_g, ffn_w_gate, ffn_w_up, ffn_conv_w, ffn_conv_b, ffn_w_down, final_norm_g):
    B, T, D = x_prompt.shape
    DB, TS, _ = x_sample.shape
    depth = ffn_norm_g.shape[0]
    F = ffn_w_gate.shape[2]
    n_attn_heads = D // (2 * ATTN_HEAD_DIM)
    n_rwkv_heads = D // RWKV_HEAD
    hd = 2 * ATTN_HEAD_DIM
    tm_p = _row_tile(T)
    tps_p = T // tm_p
    ms = TS * DB
    assert F % FFN_CHUNK == 0 and TS <= WKV_CHUNK and DB % 8 == 0
    wkv_cps = math.gcd(WKV_CHUNKS_PER_STEP, T // WKV_CHUNK)
    assert T % (WKV_CHUNK * wkv_cps) == 0

    xp = x_prompt.reshape(B * T, D)
    xs = x_sample.transpose(1, 0, 2).reshape(ms, D)

    kp_l, vp_l, ks_l, vs_l = [], [], [], []
    shp_l, shs_l, wkp_l, wks_l = [], [], [], []
    cvp_l, cvs_l = [], []
    for i in range(depth):
        j = i // 2
        if i % 2 == 0:
            lam_init = 0.8 - 0.6 * math.exp(-0.3 * i)
            lams = [v[j].reshape(1, -1) for v in (lambda_q1, lambda_k1, lambda_q2, lambda_k2)]
            wqkv = w_qkv[j].astype(BF16)
            wo = w_o_attn[j].astype(BF16)
            q0b, q1b, kb, vb, k32, v32 = _qkv(xp, attn_norm_g[j], wqkv, tm_p)
            o = _flash(q0b, q1b, kb, vb, lams, subln_g[j], B, T, lam_init)
            xp = _proj_res(xp, o, wo, tm_p)
            kp_l.append(k32.reshape(B, T, n_attn_heads, hd))
            vp_l.append(v32.reshape(B, T, n_attn_heads, hd))
            q0b, q1b, _, _, k32, v32 = _qkv(xs, attn_norm_g[j], wqkv, ms)
            qb = q0b + q1b
            k_new = k32.reshape(TS, DB, n_attn_heads, hd).transpose(1, 0, 2, 3)
            v_new = v32.reshape(TS, DB, n_attn_heads, hd).transpose(1, 0, 2, 3)
            ks_l.append(k_new)
            vs_l.append(v_new)
            q5 = qb.reshape(TS, DB, n_attn_heads, 2, ATTN_HEAD_DIM).transpose(1, 2, 3, 0, 4)
            q_all = (q5[:, :, :, :, None, :] * jnp.eye(2, dtype=BF16)[None, None, :, None, :, None])
            q_all = q_all.reshape(DB, n_attn_heads * 2 * TS, hd)
            new_pad = _round_up(TS, 16)
            pad = ((0, 0), (0, new_pad - TS), (0, 0), (0, 0))
            o = _decode(q_all, cache_k, cache_v, j, page_table, jnp.pad(k_new, pad), jnp.pad(v_new, pad),
                        lams, subln_g[j], lam_init)
            o = o.reshape(DB, n_attn_heads, 2, TS, hd)[:, :, 0].transpose(2, 0, 1, 3).reshape(ms, D)
            xs = _proj_res(xs, o.astype(BF16), wo, ms)
        else:
            lora = _round_up(w_decay1.shape[2], LANES)
            gate = _round_up(rwkv_g1.shape[2], LANES)
            w = dict(
                mix=rwkv_mix[j], rkv=w_rkv[j].astype(BF16),
                w0=w_decay0[j].reshape(1, D), d1=_pad_cols(w_decay1[j], lora).astype(BF16),
                d2=_pad_rows(w_decay2[j], lora).astype(BF16),
                a0=w_iclr0[j].reshape(1, D), i1=_pad_cols(w_iclr1[j], lora).astype(BF16),
                i2=_pad_rows(w_iclr2[j], lora).astype(BF16),
                g1=_pad_cols(rwkv_g1[j], gate).astype(BF16), g2=_pad_rows(rwkv_g2[j], gate).astype(BF16),
                k_k=k_k[j].reshape(1, D), k_a=k_a[j].reshape(1, D), r_k=r_k[j].reshape(1, D),
                lnx_g=lnx_g[j].reshape(1, D), lnx_b=lnx_b[j].reshape(1, D),
                wo=w_o_rwkv[j].astype(BF16))
            *rwkv_in, sh = _rwkv_proj(xp, rwkv_norm_g[j], jnp.zeros((B, 1, D), F32), w, tm_p, 1, tps_p)
            y, s_fin = _wkv(*rwkv_in, w, jnp.zeros((B, n_rwkv_heads, RWKV_HEAD, RWKV_HEAD), F32), WKV_CHUNK, wkv_cps)
            xp = _proj_res(xp, y, w["wo"], tm_p)
            shp_l.append(sh.reshape(B, D))
            wkp_l.append(s_fin)
            *rwkv_in, sh = _rwkv_proj(xs, rwkv_norm_g[j], state_shift[j][None], w, ms, DB, 1)

            def to_seq(a):
                a = a.reshape(TS, DB, D).transpose(1, 0, 2)
                return jnp.pad(a, ((0, 0), (0, WKV_CHUNK - TS), (0, 0))).reshape(DB * WKV_CHUNK, D)

            y, s_fin = _wkv(*(to_seq(a) for a in rwkv_in), w, state_wkv[j], WKV_CHUNK, 1)
            y = y.reshape(DB, WKV_CHUNK, D)[:, :TS].transpose(1, 0, 2).reshape(ms, D)
            xs = _proj_res(xs, y, w["wo"], ms)
            shs_l.append(sh.reshape(DB, D))
            wks_l.append(s_fin)
        w = dict(
            wg=ffn_w_gate[i].astype(BF16), wu=ffn_w_up[i].astype(BF16), cw=ffn_conv_w[i],
            cb=ffn_conv_b[i].reshape(1, F), wd=ffn_w_down[i].astype(BF16))
        final_g = final_norm_g if i == depth - 1 else None
        tm_f = min(FFN_ROW_TILE, T)
        xp, cv = _ffn(xp, ffn_norm_g[i], w, tm_f, 1, T // tm_f, None, final_g)
        cvp_l.append(cv)
        buf0 = state_ffn_conv[i].transpose(1, 0, 2).reshape(1, 2 * DB, F)
        xs, cv = _ffn(xs, ffn_norm_g[i], w, ms, DB, 1, buf0, final_g)
        cvs_l.append(cv.reshape(2, DB, F).transpose(1, 0, 2))

    y_prompt = xp.reshape(B, T, D)
    y_sample = xs.reshape(TS, DB, D).transpose(1, 0, 2)
    return (y_prompt, y_sample,
            jnp.stack(kp_l), jnp.stack(vp_l), jnp.stack(ks_l), jnp.stack(vs_l),
            jnp.stack(shp_l), jnp.stack(shs_l), jnp.stack(wkp_l), jnp.stack(wks_l),
            jnp.stack(cvp_l), jnp.stack(cvs_l))
```

```python
import functools
import math

import jax
import jax.numpy as jnp
from jax import lax
from jax.experimental import pallas as pl
from jax.experimental.pallas import tpu as pltpu

F32 = jnp.float32
BF16 = jnp.bfloat16

RMS_EPS = 1e-6
SUBLN_EPS = 1e-5
LNX_EPS = 64e-5
ATTN_HEAD_DIM = 64
RWKV_HEAD = 64
LANES = 128
NEG_BIG = -1e30
LOG2E = 1.4426950408889634
VMEM_LIMIT_BYTES = 56 * 1024 * 1024

ROW_TILE = 512
FFN_ROW_TILE = 1024
FLASH_TILE = 256
FFN_CHUNK = 256
DECODE_PAGES_PER_STEP = 16
WKV_CHUNK = 64
WKV_CHUNKS_PER_STEP = 4

NT_DIMS = (((1,), (1,)), ((), ()))
TN_DIMS = (((0,), (0,)), ((), ()))


def _params(n_axes):
    return pltpu.CompilerParams(dimension_semantics=("arbitrary",) * n_axes,
                                vmem_limit_bytes=VMEM_LIMIT_BYTES)


def _dot(a, b):
    return jnp.dot(a, b, preferred_element_type=F32)


def _dot_nt(a, b):
    return lax.dot_general(a, b, NT_DIMS, preferred_element_type=F32)


def _dot_tn(a, b):
    return lax.dot_general(a, b, TN_DIMS, preferred_element_type=F32)


def _rms(x, g, eps=RMS_EPS):
    return x * lax.rsqrt(jnp.mean(x * x, axis=-1, keepdims=True) + eps) * g


def _const_spec(shape):
    nd = len(shape)
    return pl.BlockSpec(shape, lambda *_: (0,) * nd, pipeline_mode=pl.Buffered(1))


def _row_spec(tm, d):
    return pl.BlockSpec((tm, d), lambda i: (i, 0))


def _tile_lanes(x, width):
    if width <= LANES:
        return x[:, :width]
    return jnp.concatenate([x] * (width // LANES), axis=1)


def _lambda(lq1, lk1, lq2, lk2, lam_init):
    s1 = jnp.sum(lq1[...] * lk1[...], axis=-1, keepdims=True)
    s2 = jnp.sum(lq2[...] * lk2[...], axis=-1, keepdims=True)
    return jnp.exp(s1) - jnp.exp(s2) + lam_init


def _qkv_body(x_ref, g_ref, w_ref, q0_ref, q1_ref, k_ref, v_ref, *flash_refs):
    d = x_ref.shape[1]
    h = _rms(x_ref[...], g_ref[...]).astype(BF16)
    q = (_dot(h, w_ref[:, 0:d]) * (ATTN_HEAD_DIM ** -0.5 * LOG2E)).astype(BF16)
    first_map = (lax.broadcasted_iota(jnp.int32, q.shape, 1) // ATTN_HEAD_DIM) % 2 == 0
    zero = jnp.zeros_like(q)
    q0_ref[...] = jnp.where(first_map, q, zero)
    q1_ref[...] = jnp.where(first_map, zero, q)
    k = _dot(h, w_ref[:, d:2 * d])
    k_ref[...] = k
    v = _dot(h, w_ref[:, 2 * d:3 * d])
    v_ref[...] = v
    if flash_refs:
        kb_ref, vb_ref = flash_refs
        kb_ref[...] = k.astype(BF16)
        vb_ref[...] = v.astype(BF16)


def _qkv(x, g, w_bf, tm, for_flash=False):
    m, d = x.shape
    bf = jax.ShapeDtypeStruct((m, d), BF16)
    f = jax.ShapeDtypeStruct((m, d), F32)
    out_shape = [bf, bf, f, f] + ([bf, bf] if for_flash else [])
    return pl.pallas_call(
        _qkv_body,
        grid=(m // tm,),
        in_specs=[_row_spec(tm, d), _const_spec((1, d)), _const_spec((d, 3 * d))],
        out_specs=[_row_spec(tm, d)] * len(out_shape),
        out_shape=out_shape,
        compiler_params=_params(1),
        name="qkv_proj",
    )(x, g.reshape(1, d), w_bf)


def _flash_body(q0_ref, q1_ref, k_ref, v_ref, lq1, lk1, lq2, lk2, sg_ref, x_ref, wo_ref, o_ref,
                m_sc, l_sc, acc_sc, att_sc, *, tile, n_heads, lam_init):
    qi = pl.program_id(1)
    q_start = qi * tile
    hd = 2 * ATTN_HEAD_DIM
    q_refs = (q0_ref, q1_ref)
    m_sc[...] = jnp.full(m_sc.shape, NEG_BIG, F32)
    l_sc[...] = jnp.zeros(l_sc.shape, F32)
    acc_sc[...] = jnp.zeros(acc_sc.shape, F32)

    def kv_step(k0, width, on_diagonal):
        rel = (k0 - q_start + lax.broadcasted_iota(jnp.int32, (1, width), 1)).astype(F32)
        if on_diagonal:
            keep = (lax.broadcasted_iota(jnp.int32, (tile, width), 1)
                    <= lax.broadcasted_iota(jnp.int32, (tile, width), 0))
        for h in range(n_heads):
            hs = slice(h * hd, (h + 1) * hd)
            bias = (LOG2E * 2.0 ** (-8.0 * (h + 1) / n_heads)) * rel
            kj = k_ref[pl.ds(k0, width), hs]
            vj = v_ref[pl.ds(k0, width), hs]
            for c in range(2):
                idx = 2 * h + c
                s = _dot_nt(q_refs[c][:, hs], kj) + bias
                if on_diagonal:
                    s = jnp.where(keep, s, NEG_BIG)
                m_prev = m_sc[idx]
                m_new = jnp.maximum(m_prev, jnp.max(s, axis=-1, keepdims=True))
                p = jnp.exp2(s - _tile_lanes(m_new, width))
                alpha = jnp.exp2(m_prev - m_new)
                l_sc[idx] = alpha * l_sc[idx] + jnp.sum(p, axis=-1, keepdims=True)
                acc_sc[idx] = alpha * acc_sc[idx] + _dot(p.astype(BF16), vj)
                m_sc[idx] = m_new

    def off_diagonal(j, carry):
        kv_step(pl.multiple_of(j * tile, tile), tile, False)
        return carry

    lax.fori_loop(0, qi, off_diagonal, 0)
    kv_step(pl.multiple_of(q_start, tile), tile, True)

    lam = _lambda(lq1, lk1, lq2, lk2, lam_init)
    sg = sg_ref[...]
    for h in range(n_heads):
        o = acc_sc[2 * h] / l_sc[2 * h] - lam * (acc_sc[2 * h + 1] / l_sc[2 * h + 1])
        o = o * lax.rsqrt(jnp.mean(o * o, axis=-1, keepdims=True) + SUBLN_EPS) * sg * (1.0 - lam_init)
        att_sc[:, h * hd:(h + 1) * hd] = o.astype(BF16)
    o_ref[...] = x_ref[...] + _dot(att_sc[...], wo_ref[...])


def _flash(x, q0b, q1b, kb, vb, lams, subln_g, wo_bf, n_seq, seq, lam_init):
    m, d = q0b.shape
    tile = min(FLASH_TILE, seq)
    nq = seq // tile
    hd = 2 * ATTN_HEAD_DIM
    n_heads = d // hd
    q_spec = pl.BlockSpec((tile, d), lambda b, i: (b * nq + i, 0))
    kv_spec = pl.BlockSpec((seq, d), lambda b, i: (b, 0))
    return pl.pallas_call(
        functools.partial(_flash_body, tile=tile, n_heads=n_heads, lam_init=lam_init),
        grid=(n_seq, nq),
        in_specs=[q_spec, q_spec, kv_spec, kv_spec] + [_const_spec((1, ATTN_HEAD_DIM))] * 4
                 + [_const_spec((1, hd)), q_spec, _const_spec((d, d))],
        out_specs=q_spec,
        out_shape=jax.ShapeDtypeStruct((m, d), F32),
        scratch_shapes=[pltpu.VMEM((2 * n_heads, tile, LANES), F32), pltpu.VMEM((2 * n_heads, tile, LANES), F32),
                        pltpu.VMEM((2 * n_heads, tile, hd), F32), pltpu.VMEM((tile, d), BF16)],
        compiler_params=_params(2),
        name="prompt_attention",
    )(q0b, q1b, kb, vb, *lams, subln_g.reshape(1, -1), x, wo_bf)


def _decode_body(*refs, past_len, page, n_heads, dec_seq, pages_per_step, lam_init):
    npp = pages_per_step
    q_ref = refs[1]
    kc_refs = refs[2:2 + npp]
    vc_refs = refs[2 + npp:2 + 2 * npp]
    (kn_ref, vn_ref, slope_ref, madd_ref, tok_ref, slope_n_ref, madd_n_ref, tok_n_ref,
     lq1, lk1, lq2, lk2, sg_ref, o_ref, m_sc, l_sc, acc_sc) = refs[2 + 2 * npp:]
    p = pl.program_id(1)
    n_steps = pl.num_programs(1)
    hd = 2 * ATTN_HEAD_DIM

    @pl.when(p == 0)
    def _():
        m_sc[...] = jnp.full(m_sc.shape, NEG_BIG, F32)
        l_sc[...] = jnp.zeros(l_sc.shape, F32)
        acc_sc[...] = jnp.zeros(acc_sc.shape, F32)

    q = q_ref[...]

    def update(k_list, v_list, base_bias, row_shift):
        s = [_dot_nt(q, kf.astype(BF16)) + base_bias for kf in k_list]
        m_prev = m_sc[...]
        m_new = m_prev
        for si, ri in zip(s, row_shift):
            m_new = jnp.maximum(m_new, jnp.max(si, axis=-1, keepdims=True) + ri)
        alpha = jnp.exp2(m_prev - m_new)
        l_new = alpha * l_sc[...]
        acc = alpha * acc_sc[...]
        for si, ri, vf in zip(s, row_shift, v_list):
            pm = jnp.exp2(si - _tile_lanes(m_new - ri, si.shape[1]))
            l_new = l_new + jnp.sum(pm, axis=-1, keepdims=True)
            acc = acc + _dot(pm.astype(BF16), vf.astype(BF16))
        l_sc[...] = l_new
        acc_sc[...] = acc
        m_sc[...] = m_new

    rows = page * n_heads
    slope_col = slope_ref[:, 0:LANES]
    rel = tok_ref[...] + (p * (npp * page) - past_len).astype(F32)
    update([r[...].reshape(rows, hd) for r in kc_refs], [r[...].reshape(rows, hd) for r in vc_refs],
           slope_ref[...] * rel + madd_ref[...], [slope_col * float(i * page) for i in range(npp)])

    @pl.when(p == n_steps - 1)
    def _():
        rows_n = kn_ref.shape[0] * n_heads
        update([kn_ref[...].reshape(rows_n, hd)], [vn_ref[...].reshape(rows_n, hd)],
               slope_n_ref[...] * tok_n_ref[...] + madd_n_ref[...], [0.0])
        lam = _lambda(lq1, lk1, lq2, lk2, lam_init)
        an = acc_sc[...] / l_sc[...]
        n_rows = an.shape[0]
        o = an - lam * pltpu.roll(an, n_rows - dec_seq, 0)
        o = o * lax.rsqrt(jnp.mean(o * o, axis=-1, keepdims=True) + SUBLN_EPS) * sg_ref[...] * (1.0 - lam_init)
        o_ref[...] = o


def _decode(q_all, cache_k, cache_v, layer, page_table, k_new, v_new, lams, subln_g, lam_init):
    db, n_rows, hd = q_all.shape
    _, _, page, n_heads, _ = cache_k.shape
    n_pages = page_table.shape[1]
    past_len = n_pages * page
    dec_seq = n_rows // (2 * n_heads)
    new_pad = k_new.shape[1]
    npp = max(c for c in (DECODE_PAGES_PER_STEP, 8, 4, 2, 1) if n_pages % c == 0)
    slopes = [LOG2E * 2.0 ** (-8.0 * (h + 1) / n_heads) for h in range(n_heads)]

    def consts(n_tok, causal):
        r = jnp.arange(n_rows)
        c = jnp.arange(n_tok * n_heads)
        r_head, r_tok = r // (2 * dec_seq), r % dec_seq
        c_tok, c_head = c // n_heads, c % n_heads
        ok = r_head[:, None] == c_head[None, :]
        if causal:
            ok = ok & (c_tok[None, :] <= r_tok[:, None]) & (c_tok[None, :] < dec_seq)
        slope = jnp.asarray(slopes, F32)[r_head][:, None] * jnp.ones((1, c.shape[0]), F32)
        return slope, jnp.where(ok, 0.0, NEG_BIG).astype(F32), c_tok.astype(F32)[None, :]

    slope_p, madd_p, tok_p = consts(page, False)
    slope_n, madd_n, tok_n = consts(new_pad, True)
    cols_p, cols_n = page * n_heads, new_pad * n_heads
    def cache_spec(i):
        return pl.BlockSpec((None, None, page, n_heads, hd),
                            lambda b, p, pt: (layer, pt[b, p * npp + i], 0, 0, 0))

    new_spec = pl.BlockSpec((None, new_pad, n_heads, hd), lambda b, p, pt: (b, 0, 0, 0))

    def cst(shape):
        nd = len(shape)
        return pl.BlockSpec(shape, lambda b, p, pt: (0,) * nd)

    grid_spec = pltpu.PrefetchScalarGridSpec(
        num_scalar_prefetch=1,
        grid=(db, n_pages // npp),
        in_specs=[pl.BlockSpec((None, n_rows, hd), lambda b, p, pt: (b, 0, 0))]
                 + [cache_spec(i) for i in range(npp)] * 2
                 + [new_spec, new_spec,
                    cst((n_rows, cols_p)), cst((n_rows, cols_p)), cst((1, cols_p)),
                    cst((n_rows, cols_n)), cst((n_rows, cols_n)), cst((1, cols_n))]
                 + [cst((1, ATTN_HEAD_DIM))] * 4 + [cst((1, hd))],
        out_specs=pl.BlockSpec((None, n_rows, hd), lambda b, p, pt: (b, 0, 0)),
        scratch_shapes=[pltpu.VMEM((n_rows, LANES), F32), pltpu.VMEM((n_rows, LANES), F32),
                        pltpu.VMEM((n_rows, hd), F32)],
    )
    return pl.pallas_call(
        functools.partial(_decode_body, past_len=past_len, page=page, n_heads=n_heads,
                          dec_seq=dec_seq, pages_per_step=npp, lam_init=lam_init),
        grid_spec=grid_spec,
        out_shape=jax.ShapeDtypeStruct((db, n_rows, hd), F32),
        compiler_params=_params(2),
        name="decode_attention",
    )(page_table, q_all, *([cache_k] * npp), *([cache_v] * npp), k_new, v_new, slope_p, madd_p, tok_p,
      slope_n, madd_n, tok_n, *lams, subln_g.reshape(1, -1))


def _proj_res_body(x_ref, a_ref, w_ref, o_ref):
    o_ref[...] = x_ref[...] + _dot(a_ref[...], w_ref[...])


def _proj_res(x, a_bf, w_bf, tm):
    m, d = x.shape
    return pl.pallas_call(
        _proj_res_body,
        grid=(m // tm,),
        in_specs=[_row_spec(tm, d), _row_spec(tm, d), _const_spec((d, d))],
        out_specs=_row_spec(tm, d),
        out_shape=jax.ShapeDtypeStruct((m, d), F32),
        compiler_params=_params(1),
        name="attn_out_proj",
    )(x, a_bf, w_bf)


def _shifted(h, prev, stride):
    tm = h.shape[0]
    if stride % 8 == 0:
        return jnp.concatenate([prev, h[:tm - stride]], axis=0)
    assert stride == 1
    row = lax.broadcasted_iota(jnp.int32, h.shape, 0)
    return jnp.where(row == 0, prev, pltpu.roll(h, 1, 0))


def _rwkv_proj_body(x_ref, g_ref, sh0_ref, mix_ref, wrkv_ref, w0_ref, d1_ref, d2_ref, a0_ref, i1_ref, i2_ref,
                    g1_ref, g2_ref, r_out, lw_out, k_out, v_out, a_out, g_out, sh_out,
                    carry_sc, *, stride, tiles_per_seq):
    i = pl.program_id(0)
    h = _rms(x_ref[...], g_ref[...])
    tm = h.shape[0]

    @pl.when(i % tiles_per_seq == 0)
    def _():
        carry_sc[...] = sh0_ref[...]

    hp = _shifted(h, carry_sc[...], stride)
    carry_sc[...] = h[tm - stride:]
    sh_out[...] = h[tm - stride:]
    xx = hp - h
    mix = mix_ref[...]

    def xm(s):
        return (h + xx * mix[s:s + 1]).astype(BF16)

    r_out[...] = _dot(xm(0), wrkv_ref[0])
    k_out[...] = _dot(xm(1), wrkv_ref[1])
    v_out[...] = _dot(xm(2), wrkv_ref[2])
    lora_w = _dot(jnp.tanh(_dot(xm(3), d1_ref[...])).astype(BF16), d2_ref[...])
    z = -(w0_ref[...] + lora_w)
    softplus = jnp.maximum(z, 0.0) + jnp.log(1.0 + jnp.exp(-jnp.abs(z)))
    lw_out[...] = -jnp.exp(-softplus - 0.5)
    a_out[...] = jax.nn.sigmoid(a0_ref[...] + _dot(_dot(xm(4), i1_ref[...]).astype(BF16), i2_ref[...]))
    g_out[...] = _dot(jax.nn.sigmoid(_dot(xm(5), g1_ref[...])).astype(BF16), g2_ref[...])


def _rwkv_proj(x, norm_g, shift0, w, tm, stride, tiles_per_seq):
    m, d = x.shape
    n_seq_blocks = shift0.shape[0]
    row = _row_spec(tm, d)
    f = jax.ShapeDtypeStruct((m, d), F32)
    vec = _const_spec((1, d))
    seq_spec = pl.BlockSpec((None, stride, d), lambda i: (i // tiles_per_seq, 0, 0))
    lora = w["d1"].shape[1]
    gate = w["g1"].shape[1]
    return pl.pallas_call(
        functools.partial(_rwkv_proj_body, stride=stride, tiles_per_seq=tiles_per_seq),
        grid=(m // tm,),
        in_specs=[row, vec, seq_spec, _const_spec((6, d)), _const_spec((3, d, d)),
                  vec, _const_spec((d, lora)), _const_spec((lora, d)),
                  vec, _const_spec((d, lora)), _const_spec((lora, d)),
                  _const_spec((d, gate)), _const_spec((gate, d))],
        out_specs=[row] * 6 + [seq_spec],
        out_shape=[f] * 6 + [jax.ShapeDtypeStruct((n_seq_blocks, stride, d), F32)],
        scratch_shapes=[pltpu.VMEM((stride, d), F32)],
        compiler_params=_params(1),
        name="rwkv_proj",
    )(x, norm_g.reshape(1, d), shift0, w["mix"], w["rkv"], w["w0"], w["d1"], w["d2"], w["a0"], w["i1"], w["i2"],
      w["g1"], w["g2"])


def _head_sums(x):
    rows, d = x.shape
    assert LANES == 2 * RWKV_HEAD and d % LANES == 0
    low = lax.broadcasted_iota(jnp.int32, (rows, LANES), 1) < RWKV_HEAD
    out = []
    for c in range(d // LANES):
        xc = x[:, c * LANES:(c + 1) * LANES]
        s_lo = jnp.sum(jnp.where(low, xc, 0.0), axis=-1, keepdims=True)
        s_hi = jnp.sum(jnp.where(low, 0.0, xc), axis=-1, keepdims=True)
        out.append(jnp.where(low, s_lo, s_hi))
    return jnp.concatenate(out, axis=1)


def _wkv_body(*refs, n_heads, chunk, project):
    chunk_refs, s0_ref = refs[:11], refs[11]
    if project:
        x_ref, wo_ref, o_ref, sout_ref, s_sc, y_sc, gated_sc = refs[12:]
    else:
        o_ref, sout_ref, s_sc, y_sc = refs[12:]
        gated_sc = o_ref
    c = pl.program_id(1)

    @pl.when(c == 0)
    def _():
        s_sc[...] = s0_ref[...]

    state = [s_sc[h] for h in range(n_heads)]
    for sub in range(y_sc.shape[0] // chunk):
        rows = slice(sub * chunk, (sub + 1) * chunk)
        state = _wkv_chunk(rows, *chunk_refs, gated_sc, y_sc, state)
    for h in range(n_heads):
        s_sc[h] = state[h]
    if project:
        o_ref[...] = x_ref[...] + _dot(gated_sc[...], wo_ref[...])

    @pl.when(c == pl.num_programs(1) - 1)
    def _():
        sout_ref[...] = s_sc[...]


def _wkv_chunk(rows, r_ref, lw_ref, k_ref, v_ref, ia_ref, g_ref, kk_ref, ka_ref, rk_ref, lg_ref, lb_ref,
               o_ref, y_sc, s_old):
    L = rows.stop - rows.start
    N = RWKV_HEAD
    n_heads = len(s_old)

    iclr = ia_ref[rows, :]
    k_raw = k_ref[rows, :]
    kk = k_raw * kk_ref[...]
    kk = kk / jnp.maximum(jnp.sqrt(_head_sums(kk * kk)), 1e-12)
    a_in = -kk
    b_in = kk * iclr
    k_in = k_raw * (1.0 + (iclr - 1.0) * ka_ref[...])
    r_in = r_ref[rows, :]
    v_in = v_ref[rows, :]
    bonus = _head_sums(r_in * k_in * rk_ref[...]) * v_in

    row = lax.broadcasted_iota(jnp.int32, (L, L), 0)
    col = lax.broadcasted_iota(jnp.int32, (L, L), 1)
    tri = (row >= col).astype(F32).astype(BF16)
    eye = (row == col).astype(F32)
    row2 = lax.broadcasted_iota(jnp.int32, (2 * L, 2 * L), 0)
    tok2 = lax.broadcasted_iota(jnp.int32, (2 * L, 2 * L), 1) % L
    gram_keep = tok2 < jnp.where(row2 < L, row2, row2 - L + 1)

    lw = lw_ref[rows, :]
    hi = lw.astype(BF16)
    r1 = lw - hi.astype(F32)
    mid = r1.astype(BF16)
    lo = (r1 - mid.astype(F32)).astype(BF16)
    cum = _dot(tri, hi) + _dot(tri, mid) + _dot(tri, lo)
    wl = cum[L - 1:L]
    e_in = jnp.exp(cum)
    e_out = jnp.exp(-cum)
    e_tail = jnp.exp(wl - cum)
    at = (a_in * jnp.exp(cum - lw)).astype(BF16)
    rt = (r_in * e_in).astype(BF16)
    bt = (b_in * e_out).astype(BF16)
    kt = (k_in * e_out).astype(BF16)
    bw = (b_in * e_tail).astype(BF16)
    kw = (k_in * e_tail).astype(BF16)
    vb = v_in.astype(BF16)
    ewl = jnp.exp(wl)

    heads = range(n_heads)
    hs = [slice(h * N, (h + 1) * N) for h in heads]
    ar = [jnp.concatenate([at[:, s], rt[:, s]], axis=0) for s in hs]
    bk_t = [jnp.concatenate([bt[:, s], kt[:, s]], axis=0) for s in hs]
    gram = [jnp.where(gram_keep, _dot_nt(ar[h], bk_t[h]), 0.0) for h in heads]
    p_s = [_dot_nt(ar[h], s_old[h].astype(BF16)) for h in heads]
    a_ab = [g[:L, :L] for g in gram]
    zero_v = jnp.zeros((L, N), BF16)
    rhs = [(p_s[h][:L] + _dot(gram[h][:L].astype(BF16), jnp.concatenate([zero_v, vb[:, hs[h]]], axis=0))
            ).astype(BF16) for h in heads]

    t_inv = [eye + a for a in a_ab]
    a_pow = []
    for a in a_ab:
        ab = a.astype(BF16)
        a_pow.append(_dot(ab, ab))
    n_steps = int(math.log2(L)) - 1
    for step in range(n_steps):
        last = step == n_steps - 1
        for h in heads:
            pb = a_pow[h].astype(BF16)
            if last:
                t_inv[h] = t_inv[h] + _dot(t_inv[h].astype(BF16), pb)
            else:
                z = _dot(jnp.concatenate([t_inv[h], a_pow[h]], axis=0).astype(BF16), pb)
                t_inv[h] = t_inv[h] + z[:L]
                a_pow[h] = z[L:]

    u = [_dot(t_inv[h].astype(BF16), rhs[h]).astype(BF16) for h in heads]
    uv_l = [jnp.concatenate([u[h], vb[:, hs[h]]], axis=0) for h in heads]
    for h in heads:
        y_sc[rows, hs[h]] = p_s[h][L:] + _dot(gram[h][L:].astype(BF16), uv_l[h])
    s_new = []
    for h in heads:
        uv = uv_l[h]
        bk = jnp.concatenate([bw[:, hs[h]], kw[:, hs[h]]], axis=0)
        s_new.append(s_old[h] * ewl[:, hs[h]] + _dot_tn(uv, bk))

    y = y_sc[rows, :]
    inv_n = 1.0 / N
    yc = y - _head_sums(y) * inv_n
    var = _head_sums(yc * yc) * inv_n
    yn = yc * lax.rsqrt(var + LNX_EPS) * lg_ref[...] + lb_ref[...]
    o_ref[rows, :] = ((yn + bonus) * g_ref[rows, :]).astype(BF16)
    return s_new


def _wkv(r, lw, k, v, iclr, g, w, s0, chunk, chunks_per_step, x=None):
    m, d = r.shape
    n_seq, n_heads = s0.shape[0], s0.shape[1]
    step_rows = chunk * chunks_per_step
    n_chunks = m // (n_seq * step_rows)
    row = pl.BlockSpec((step_rows, d), lambda s, c: (s * n_chunks + c, 0))
    st = pl.BlockSpec((None, n_heads, RWKV_HEAD, RWKV_HEAD), lambda s, c: (s, 0, 0, 0))
    vec = _const_spec((1, d))
    project = x is not None
    ins = [r, lw, k, v, iclr, g, w["k_k"], w["k_a"], w["r_k"], w["lnx_g"], w["lnx_b"], s0]
    in_specs = [row] * 6 + [vec] * 5 + [st]
    scratch = [pltpu.VMEM((n_heads, RWKV_HEAD, RWKV_HEAD), F32), pltpu.VMEM((step_rows, d), F32)]
    if project:
        ins += [x, w["wo"]]
        in_specs += [row, _const_spec((d, d))]
        scratch.append(pltpu.VMEM((step_rows, d), BF16))
    return pl.pallas_call(
        functools.partial(_wkv_body, n_heads=n_heads, chunk=chunk, project=project),
        grid=(n_seq, n_chunks),
        in_specs=in_specs,
        out_specs=[row, st],
        out_shape=[jax.ShapeDtypeStruct((m, d), F32 if project else BF16), jax.ShapeDtypeStruct(s0.shape, F32)],
        scratch_shapes=scratch,
        compiler_params=_params(2),
        name="wkv7_chunked",
    )(*ins)


def _ffn_body(*refs, stride, tiles_per_seq, n_chunks, has_buf, final):
    x_ref, g_ref, wg_ref, wu_ref, cw_ref, cb_ref, wd_ref = refs[:7]
    pos = 7
    buf_ref = gf_ref = None
    if has_buf:
        buf_ref = refs[pos]
        pos += 1
    if final:
        gf_ref = refs[pos]
        pos += 1
    o_ref, cv_ref, carry_sc, acc_sc = refs[pos:pos + 4]
    i = pl.program_id(0)
    x = x_ref[...]
    tm = x.shape[0]
    fc = wg_ref.shape[1] // n_chunks
    h = _rms(x, g_ref[...]).astype(BF16)
    acc_sc[...] = jnp.zeros(acc_sc.shape, F32)
    if not has_buf:
        @pl.when(i % tiles_per_seq == 0)
        def _():
            carry_sc[...] = jnp.zeros(carry_sc.shape, F32)
        row8 = lax.broadcasted_iota(jnp.int32, (8, fc), 0)

    for c in range(n_chunks):
        cs = slice(c * fc, (c + 1) * fc)
        u = _dot(h, wg_ref[:, cs])
        up = _dot(h, wu_ref[:, cs])
        cw = cw_ref[:, cs]
        if has_buf:
            full = jnp.concatenate([buf_ref[:, cs], u], axis=0)
            u2 = full[0:tm]
            u1 = full[stride:stride + tm]
            cv_ref[:, cs] = full[tm:tm + 2 * stride]
        else:
            prev = carry_sc[:, cs]
            r1 = pltpu.roll(u, 1, 0)
            r2 = pltpu.roll(u, 2, 0)
            u1 = jnp.concatenate([jnp.where(row8 == 0, prev[1:2], r1[:8]), r1[8:]], axis=0)
            u2 = jnp.concatenate(
                [jnp.where(row8 == 0, prev[0:1], jnp.where(row8 == 1, prev[1:2], r2[:8])), r2[8:]], axis=0)
            carry_sc[:, cs] = u[tm - 2:tm]
            cv_ref[:, cs] = u[tm - 2:tm]
        cv = cb_ref[:, cs] + u2 * cw[0:1] + u1 * cw[1:2] + u * cw[2:3]
        half = 0.5 * cv
        act = (half * (1.0 + jnp.tanh(half)) * up).astype(BF16)
        acc_sc[...] += _dot(act, wd_ref[cs, :])
    out = x + acc_sc[...]
    if final:
        out = _rms(out, gf_ref[...])
    o_ref[...] = out


def _ffn(x, norm_g, w, tm, stride, tiles_per_seq, buf0, final_g):
    m, d = x.shape
    f = w["wg"].shape[1]
    n_chunks = f // FFN_CHUNK
    has_buf = buf0 is not None
    final = final_g is not None
    n_seq_blocks = (m // tm) // tiles_per_seq
    keep = 2 * stride
    seq_spec = pl.BlockSpec((None, keep, f), lambda i: (i // tiles_per_seq, 0, 0))
    ins = [x, norm_g.reshape(1, d), w["wg"], w["wu"], w["cw"], w["cb"], w["wd"]]
    in_specs = [_row_spec(tm, d), _const_spec((1, d)), _const_spec((d, f)), _const_spec((d, f)),
                _const_spec((3, f)), _const_spec((1, f)), _const_spec((f, d))]
    if has_buf:
        ins.append(buf0)
        in_specs.append(seq_spec)
    if final:
        ins.append(final_g.reshape(1, d))
        in_specs.append(_const_spec((1, d)))
    return pl.pallas_call(
        functools.partial(_ffn_body, stride=stride, tiles_per_seq=tiles_per_seq, n_chunks=n_chunks,
                          has_buf=has_buf, final=final),
        grid=(m // tm,),
        in_specs=in_specs,
        out_specs=[_row_spec(tm, d), seq_spec],
        out_shape=[jax.ShapeDtypeStruct((m, d), F32),
                   jax.ShapeDtypeStruct((n_seq_blocks, keep, f), F32)],
        scratch_shapes=[pltpu.VMEM((2, f), F32), pltpu.VMEM((tm, d), F32)],
        compiler_params=_params(1),
        name="conv_ffn",
    )(*ins)


def _pad_cols(w, n):
    return jnp.pad(w, ((0, 0), (0, n - w.shape[1])))


def _pad_rows(w, n):
    return jnp.pad(w, ((0, n - w.shape[0]), (0, 0)))


def _round_up(n, mult):
    return -(-n // mult) * mult


def _row_tile(rows):
    tm = min(ROW_TILE, rows)
    assert rows % tm == 0
    return tm


def kernel(x_prompt, x_sample, cache_k, cache_v, page_table, state_shift, state_wkv, state_ffn_conv, attn_norm_g, w_qkv, lambda_q1, lambda_k1, lambda_q2, lambda_k2, subln_g, w_o_attn, rwkv_norm_g, rwkv_mix, w_rkv, w_decay0, w_decay1, w_decay2, w_iclr0, w_iclr1, w_iclr2, rwkv_g1, rwkv_g2, k_k, k_a, r_k, lnx_g, lnx_b, w_o_rwkv, ffn_norm_g, ffn_w_gate, ffn_w_up, ffn_conv_w, ffn_conv_b, ffn_w_down, final_norm_g):
    B, T, D = x_prompt.shape
    DB, TS, _ = x_sample.shape
    depth = ffn_norm_g.shape[0]
    F = ffn_w_gate.shape[2]
    n_attn_heads = D // (2 * ATTN_HEAD_DIM)
    n_rwkv_heads = D // RWKV_HEAD
    hd = 2 * ATTN_HEAD_DIM
    tm_p = _row_tile(T)
    tps_p = T // tm_p
    ms = TS * DB
    assert F % FFN_CHUNK == 0 and DB % 8 == 0
    wkv_cps = math.gcd(WKV_CHUNKS_PER_STEP, T // WKV_CHUNK)
    assert T % (WKV_CHUNK * wkv_cps) == 0

    xp = x_prompt.reshape(B * T, D)
    xs = x_sample.transpose(1, 0, 2).reshape(ms, D)

    kp_l, vp_l, ks_l, vs_l = [], [], [], []
    shp_l, shs_l, wkp_l, wks_l = [], [], [], []
    cvp_l, cvs_l = [], []
    for i in range(depth):
        j = i // 2
        if i % 2 == 0:
            lam_init = 0.8 - 0.6 * math.exp(-0.3 * i)
            lams = [v[j].reshape(1, -1) for v in (lambda_q1, lambda_k1, lambda_q2, lambda_k2)]
            wqkv = w_qkv[j].astype(BF16)
            wo = w_o_attn[j].astype(BF16)
            q0b, q1b, k32, v32, kb, vb = _qkv(xp, attn_norm_g[j], wqkv, tm_p, for_flash=True)
            xp = _flash(xp, q0b, q1b, kb, vb, lams, subln_g[j], wo, B, T, lam_init)
            kp_l.append(k32.reshape(B, T, n_attn_heads, hd))
            vp_l.append(v32.reshape(B, T, n_attn_heads, hd))
            q0b, q1b, k32, v32 = _qkv(xs, attn_norm_g[j], wqkv, ms)
            qb = q0b + q1b
            k_new = k32.reshape(TS, DB, n_attn_heads, hd).transpose(1, 0, 2, 3)
            v_new = v32.reshape(TS, DB, n_attn_heads, hd).transpose(1, 0, 2, 3)
            ks_l.append(k_new)
            vs_l.append(v_new)
            q5 = qb.reshape(TS, DB, n_attn_heads, 2, ATTN_HEAD_DIM).transpose(1, 2, 3, 0, 4)
            q_all = (q5[:, :, :, :, None, :] * jnp.eye(2, dtype=BF16)[None, None, :, None, :, None])
            q_all = q_all.reshape(DB, n_attn_heads * 2 * TS, hd)
            new_pad = _round_up(TS, 16)
            pad = ((0, 0), (0, new_pad - TS), (0, 0), (0, 0))
            o = _decode(q_all, cache_k, cache_v, j, page_table, jnp.pad(k_new, pad), jnp.pad(v_new, pad),
                        lams, subln_g[j], lam_init)
            o = o.reshape(DB, n_attn_heads, 2, TS, hd)[:, :, 0].transpose(2, 0, 1, 3).reshape(ms, D)
            xs = _proj_res(xs, o.astype(BF16), wo, ms)
        else:
            lora = _round_up(w_decay1.shape[2], LANES)
            gate = _round_up(rwkv_g1.shape[2], LANES)
            w = dict(
                mix=rwkv_mix[j], rkv=w_rkv[j].astype(BF16),
                w0=w_decay0[j].reshape(1, D), d1=_pad_cols(w_decay1[j], lora).astype(BF16),
                d2=_pad_rows(w_decay2[j], lora).astype(BF16),
                a0=w_iclr0[j].reshape(1, D), i1=_pad_cols(w_iclr1[j], lora).astype(BF16),
                i2=_pad_rows(w_iclr2[j], lora).astype(BF16),
                g1=_pad_cols(rwkv_g1[j], gate).astype(BF16), g2=_pad_rows(rwkv_g2[j], gate).astype(BF16),
                k_k=k_k[j].reshape(1, D), k_a=k_a[j].reshape(1, D), r_k=r_k[j].reshape(1, D),
                lnx_g=lnx_g[j].reshape(1, D), lnx_b=lnx_b[j].reshape(1, D),
                wo=w_o_rwkv[j].astype(BF16))
            *rwkv_in, sh = _rwkv_proj(xp, rwkv_norm_g[j], jnp.zeros((B, 1, D), F32), w, tm_p, 1, tps_p)
            xp, s_fin = _wkv(*rwkv_in, w, jnp.zeros((B, n_rwkv_heads, RWKV_HEAD, RWKV_HEAD), F32), WKV_CHUNK,
                             wkv_cps, x=xp)
            shp_l.append(sh.reshape(B, D))
            wkp_l.append(s_fin)
            *rwkv_in, sh = _rwkv_proj(xs, rwkv_norm_g[j], state_shift[j][None], w, ms, DB, 1)

            chunk_s = max(16, pl.next_power_of_2(TS))

            def to_seq(a):
                a = a.reshape(TS, DB, D).transpose(1, 0, 2)
                return jnp.pad(a, ((0, 0), (0, chunk_s - TS), (0, 0))).reshape(DB * chunk_s, D)

            y, s_fin = _wkv(*(to_seq(a) for a in rwkv_in), w, state_wkv[j], chunk_s, 1)
            y = y.reshape(DB, chunk_s, D)[:, :TS].transpose(1, 0, 2).reshape(ms, D)
            xs = _proj_res(xs, y, w["wo"], ms)
            shs_l.append(sh.reshape(DB, D))
            wks_l.append(s_fin)
        w = dict(
            wg=ffn_w_gate[i].astype(BF16), wu=ffn_w_up[i].astype(BF16), cw=ffn_conv_w[i],
            cb=ffn_conv_b[i].reshape(1, F), wd=ffn_w_down[i].astype(BF16))
        final_g = final_norm_g if i == depth - 1 else None
        tm_f = min(FFN_ROW_TILE, T)
        xp, cv = _ffn(xp, ffn_norm_g[i], w, tm_f, 1, T // tm_f, None, final_g)
        cvp_l.append(cv)
        buf0 = state_ffn_conv[i].transpose(1, 0, 2).reshape(1, 2 * DB, F)
        xs, cv = _ffn(xs, ffn_norm_g[i], w, ms, DB, 1, buf0, final_g)
        cvs_l.append(cv.reshape(2, DB, F).transpose(1, 0, 2))

    y_prompt = xp.reshape(B, T, D)
    y_sample = xs.reshape(TS, DB, D).transpose(1, 0, 2)
    return (y_prompt, y_sample,
            jnp.stack(kp_l), jnp.stack(vp_l), jnp.stack(ks_l), jnp.stack(vs_l),
            jnp.stack(shp_l), jnp.stack(shs_l), jnp.stack(wkp_l), jnp.stack(wks_l),
            jnp.stack(cvp_l), jnp.stack(cvs_l))
```

```python
import functools
import math

import jax
import jax.numpy as jnp
import numpy as np
from jax import lax
from jax.experimental import pallas as pl
from jax.experimental.pallas import tpu as pltpu

F32 = jnp.float32
BF16 = jnp.bfloat16

RMS_EPS = 1e-6
SUBLN_EPS = 1e-5
LNX_EPS = 64e-5
ATTN_HEAD_DIM = 64
RWKV_HEAD = 64
LANES = 128
NEG_BIG = -1e30
LOG2E = 1.4426950408889634
VMEM_LIMIT_BYTES = 56 * 1024 * 1024

ROW_TILE = 512
FFN_ROW_TILE = 1024
FLASH_TILE = 256
FFN_CHUNK = 256
DECODE_PAGES_PER_STEP = 16
DECODE_GROUP = 16
WKV_CHUNK = 64
WKV_CHUNKS_PER_STEP = 8

NT_DIMS = (((1,), (1,)), ((), ()))
TN_DIMS = (((0,), (0,)), ((), ()))


def _params(n_axes):
    return pltpu.CompilerParams(dimension_semantics=("arbitrary",) * n_axes,
                                vmem_limit_bytes=VMEM_LIMIT_BYTES)


def _dot(a, b):
    return jnp.dot(a, b, preferred_element_type=F32)


def _dot_nt(a, b):
    return lax.dot_general(a, b, NT_DIMS, preferred_element_type=F32)


def _dot_tn(a, b):
    return lax.dot_general(a, b, TN_DIMS, preferred_element_type=F32)


def _rms(x, g, eps=RMS_EPS):
    return x * lax.rsqrt(jnp.mean(x * x, axis=-1, keepdims=True) + eps) * g


def _const_spec(shape):
    nd = len(shape)
    return pl.BlockSpec(shape, lambda *_: (0,) * nd, pipeline_mode=pl.Buffered(1))


def _row_spec(tm, d):
    return pl.BlockSpec((tm, d), lambda i: (i, 0))


def _tile_lanes(x, width):
    if width <= LANES:
        return x[:, :width]
    return jnp.concatenate([x] * (width // LANES), axis=1)


def _lambda(lq1, lk1, lq2, lk2, lam_init):
    s1 = jnp.sum(lq1[...] * lk1[...], axis=-1, keepdims=True)
    s2 = jnp.sum(lq2[...] * lk2[...], axis=-1, keepdims=True)
    return jnp.exp(s1) - jnp.exp(s2) + lam_init


def _qkv_body(x_ref, g_ref, w_ref, q0_ref, q1_ref, k_ref, v_ref, *flash_refs):
    d = x_ref.shape[1]
    h = _rms(x_ref[...], g_ref[...]).astype(BF16)
    q = (_dot(h, w_ref[:, 0:d]) * (ATTN_HEAD_DIM ** -0.5 * LOG2E)).astype(BF16)
    first_map = (lax.broadcasted_iota(jnp.int32, q.shape, 1) // ATTN_HEAD_DIM) % 2 == 0
    zero = jnp.zeros_like(q)
    q0_ref[...] = jnp.where(first_map, q, zero)
    q1_ref[...] = jnp.where(first_map, zero, q)
    k = _dot(h, w_ref[:, d:2 * d])
    k_ref[...] = k
    v = _dot(h, w_ref[:, 2 * d:3 * d])
    v_ref[...] = v
    if flash_refs:
        kb_ref, vb_ref = flash_refs
        kb_ref[...] = k.astype(BF16)
        vb_ref[...] = v.astype(BF16)


def _qkv(x, g, w_bf, tm, for_flash=False):
    m, d = x.shape
    bf = jax.ShapeDtypeStruct((m, d), BF16)
    f = jax.ShapeDtypeStruct((m, d), F32)
    out_shape = [bf, bf, f, f] + ([bf, bf] if for_flash else [])
    return pl.pallas_call(
        _qkv_body,
        grid=(m // tm,),
        in_specs=[_row_spec(tm, d), _const_spec((1, d)), _const_spec((d, 3 * d))],
        out_specs=[_row_spec(tm, d)] * len(out_shape),
        out_shape=out_shape,
        compiler_params=_params(1),
        name="qkv_proj",
    )(x, g.reshape(1, d), w_bf)


def _flash_body(q0_ref, q1_ref, k_ref, v_ref, lq1, lk1, lq2, lk2, sg_ref, x_ref, wo_ref, o_ref,
                m_sc, l_sc, acc_sc, att_sc, *, tile, n_heads, lam_init):
    qi = pl.program_id(1)
    q_start = qi * tile
    hd = 2 * ATTN_HEAD_DIM
    q_refs = (q0_ref, q1_ref)
    m_sc[...] = jnp.full(m_sc.shape, NEG_BIG, F32)
    l_sc[...] = jnp.zeros(l_sc.shape, F32)
    acc_sc[...] = jnp.zeros(acc_sc.shape, F32)

    def kv_step(k0, width, on_diagonal):
        rel = (k0 - q_start + lax.broadcasted_iota(jnp.int32, (1, width), 1)).astype(F32)
        if on_diagonal:
            keep = (lax.broadcasted_iota(jnp.int32, (tile, width), 1)
                    <= lax.broadcasted_iota(jnp.int32, (tile, width), 0))
        for h in range(n_heads):
            hs = slice(h * hd, (h + 1) * hd)
            bias = (LOG2E * 2.0 ** (-8.0 * (h + 1) / n_heads)) * rel
            kj = k_ref[pl.ds(k0, width), hs]
            vj = v_ref[pl.ds(k0, width), hs]
            for c in range(2):
                idx = 2 * h + c
                s = _dot_nt(q_refs[c][:, hs], kj) + bias
                if on_diagonal:
                    s = jnp.where(keep, s, NEG_BIG)
                m_prev = m_sc[idx]
                m_new = jnp.maximum(m_prev, jnp.max(s, axis=-1, keepdims=True))
                p = jnp.exp2(s - _tile_lanes(m_new, width))
                alpha = jnp.exp2(m_prev - m_new)
                l_sc[idx] = alpha * l_sc[idx] + jnp.sum(p, axis=-1, keepdims=True)
                acc_sc[idx] = alpha * acc_sc[idx] + _dot(p.astype(BF16), vj)
                m_sc[idx] = m_new

    def off_diagonal(j, carry):
        kv_step(pl.multiple_of(j * tile, tile), tile, False)
        return carry

    lax.fori_loop(0, qi, off_diagonal, 0)
    kv_step(pl.multiple_of(q_start, tile), tile, True)

    lam = _lambda(lq1, lk1, lq2, lk2, lam_init)
    sg = sg_ref[...]
    for h in range(n_heads):
        o = acc_sc[2 * h] / l_sc[2 * h] - lam * (acc_sc[2 * h + 1] / l_sc[2 * h + 1])
        o = o * lax.rsqrt(jnp.mean(o * o, axis=-1, keepdims=True) + SUBLN_EPS) * sg * (1.0 - lam_init)
        att_sc[:, h * hd:(h + 1) * hd] = o.astype(BF16)
    o_ref[...] = x_ref[...] + _dot(att_sc[...], wo_ref[...])


def _flash(x, q0b, q1b, kb, vb, lams, subln_g, wo_bf, n_seq, seq, lam_init):
    m, d = q0b.shape
    tile = min(FLASH_TILE, seq)
    nq = seq // tile
    hd = 2 * ATTN_HEAD_DIM
    n_heads = d // hd
    q_spec = pl.BlockSpec((tile, d), lambda b, i: (b * nq + i, 0))
    kv_spec = pl.BlockSpec((seq, d), lambda b, i: (b, 0))
    return pl.pallas_call(
        functools.partial(_flash_body, tile=tile, n_heads=n_heads, lam_init=lam_init),
        grid=(n_seq, nq),
        in_specs=[q_spec, q_spec, kv_spec, kv_spec] + [_const_spec((1, ATTN_HEAD_DIM))] * 4
                 + [_const_spec((1, hd)), q_spec, _const_spec((d, d))],
        out_specs=q_spec,
        out_shape=jax.ShapeDtypeStruct((m, d), F32),
        scratch_shapes=[pltpu.VMEM((2 * n_heads, tile, LANES), F32), pltpu.VMEM((2 * n_heads, tile, LANES), F32),
                        pltpu.VMEM((2 * n_heads, tile, hd), F32), pltpu.VMEM((tile, d), BF16)],
        compiler_params=_params(2),
        name="prompt_attention",
    )(q0b, q1b, kb, vb, *lams, subln_g.reshape(1, -1), x, wo_bf)


def _decode_body(*refs, past_len, page, n_heads, dec_seq, pages_per_step, lam_init):
    npp = pages_per_step
    q_ref = refs[1]
    kc_refs = refs[2:2 + npp]
    vc_refs = refs[2 + npp:2 + 2 * npp]
    (kn_ref, vn_ref, slope_ref, madd_ref, tok_ref, slope_n_ref, madd_n_ref, tok_n_ref,
     lq1, lk1, lq2, lk2, sg_ref, o_ref, m_sc, l_sc, acc_sc) = refs[2 + 2 * npp:]
    p = pl.program_id(1)
    n_steps = pl.num_programs(1)
    hd = 2 * ATTN_HEAD_DIM

    @pl.when(p == 0)
    def _():
        m_sc[...] = jnp.full(m_sc.shape, NEG_BIG, F32)
        l_sc[...] = jnp.zeros(l_sc.shape, F32)
        acc_sc[...] = jnp.zeros(acc_sc.shape, F32)

    q = q_ref[...]

    def update(k_refs, v_refs, n_rows_kv, base_bias, row_shift):
        m_run, l_run, acc = m_sc[...], l_sc[...], acc_sc[...]
        for g in range(0, len(k_refs), DECODE_GROUP):
            grp = range(g, min(g + DECODE_GROUP, len(k_refs)))
            s = [_dot_nt(q, k_refs[i][...].reshape(n_rows_kv, hd).astype(BF16)) + base_bias for i in grp]
            m_new = m_run
            for si, i in zip(s, grp):
                m_new = jnp.maximum(m_new, jnp.max(si, axis=-1, keepdims=True) + row_shift[i])
            alpha = jnp.exp2(m_run - m_new)
            l_run = alpha * l_run
            acc = alpha * acc
            for si, i in zip(s, grp):
                pm = jnp.exp2(si - _tile_lanes(m_new - row_shift[i], si.shape[1]))
                l_run = l_run + jnp.sum(pm, axis=-1, keepdims=True)
                acc = acc + _dot(pm.astype(BF16), v_refs[i][...].reshape(n_rows_kv, hd).astype(BF16))
            m_run = m_new
        l_sc[...] = l_run
        acc_sc[...] = acc
        m_sc[...] = m_run

    rows = page * n_heads
    slope_col = slope_ref[:, 0:LANES]
    rel = tok_ref[...] + (p * (npp * page) - past_len).astype(F32)
    update(kc_refs, vc_refs, rows, slope_ref[...] * rel + madd_ref[...],
           [slope_col * float(i * page) for i in range(npp)])

    @pl.when(p == n_steps - 1)
    def _():
        update([kn_ref], [vn_ref], kn_ref.shape[0] * n_heads,
               slope_n_ref[...] * tok_n_ref[...] + madd_n_ref[...], [0.0])
        lam = _lambda(lq1, lk1, lq2, lk2, lam_init)
        an = acc_sc[...] / l_sc[...]
        n_rows = an.shape[0]
        o = an - lam * pltpu.roll(an, n_rows - dec_seq, 0)
        o = o * lax.rsqrt(jnp.mean(o * o, axis=-1, keepdims=True) + SUBLN_EPS) * sg_ref[...] * (1.0 - lam_init)
        o_ref[...] = o


def _decode(q_all, cache_k, cache_v, layer, page_table, k_new, v_new, lams, subln_g, lam_init):
    db, n_rows, hd = q_all.shape
    _, _, page, n_heads, _ = cache_k.shape
    n_pages = page_table.shape[1]
    past_len = n_pages * page
    dec_seq = n_rows // (2 * n_heads)
    new_pad = k_new.shape[1]
    npp = max(c for c in (DECODE_PAGES_PER_STEP, 8, 4, 2, 1) if n_pages % c == 0)
    slopes = [LOG2E * 2.0 ** (-8.0 * (h + 1) / n_heads) for h in range(n_heads)]

    def consts(n_tok, causal):
        r = np.arange(n_rows)
        c = np.arange(n_tok * n_heads)
        r_head, r_tok = r // (2 * dec_seq), r % dec_seq
        c_tok, c_head = c // n_heads, c % n_heads
        ok = r_head[:, None] == c_head[None, :]
        if causal:
            ok = ok & (c_tok[None, :] <= r_tok[:, None]) & (c_tok[None, :] < dec_seq)
        slope = np.asarray(slopes, np.float32)[r_head][:, None] * np.ones((1, c.shape[0]), np.float32)
        return slope, np.where(ok, 0.0, NEG_BIG).astype(np.float32), c_tok.astype(np.float32)[None, :]

    slope_p, madd_p, tok_p = consts(page, False)
    slope_n, madd_n, tok_n = consts(new_pad, True)
    cols_p, cols_n = page * n_heads, new_pad * n_heads
    def cache_spec(i):
        return pl.BlockSpec((None, None, page, n_heads, hd),
                            lambda b, p, pt: (layer, pt[b, p * npp + i], 0, 0, 0))

    new_spec = pl.BlockSpec((None, new_pad, n_heads, hd), lambda b, p, pt: (b, 0, 0, 0))

    def cst(shape):
        nd = len(shape)
        return pl.BlockSpec(shape, lambda b, p, pt: (0,) * nd)

    grid_spec = pltpu.PrefetchScalarGridSpec(
        num_scalar_prefetch=1,
        grid=(db, n_pages // npp),
        in_specs=[pl.BlockSpec((None, n_rows, hd), lambda b, p, pt: (b, 0, 0))]
                 + [cache_spec(i) for i in range(npp)] * 2
                 + [new_spec, new_spec,
                    cst((n_rows, cols_p)), cst((n_rows, cols_p)), cst((1, cols_p)),
                    cst((n_rows, cols_n)), cst((n_rows, cols_n)), cst((1, cols_n))]
                 + [cst((1, ATTN_HEAD_DIM))] * 4 + [cst((1, hd))],
        out_specs=pl.BlockSpec((None, n_rows, hd), lambda b, p, pt: (b, 0, 0)),
        scratch_shapes=[pltpu.VMEM((n_rows, LANES), F32), pltpu.VMEM((n_rows, LANES), F32),
                        pltpu.VMEM((n_rows, hd), F32)],
    )
    return pl.pallas_call(
        functools.partial(_decode_body, past_len=past_len, page=page, n_heads=n_heads,
                          dec_seq=dec_seq, pages_per_step=npp, lam_init=lam_init),
        grid_spec=grid_spec,
        out_shape=jax.ShapeDtypeStruct((db, n_rows, hd), F32),
        compiler_params=_params(2),
        name="decode_attention",
    )(page_table, q_all, *([cache_k] * npp), *([cache_v] * npp), k_new, v_new, slope_p, madd_p, tok_p,
      slope_n, madd_n, tok_n, *lams, subln_g.reshape(1, -1))


def _proj_res_body(x_ref, a_ref, w_ref, o_ref):
    o_ref[...] = x_ref[...] + _dot(a_ref[...], w_ref[...])


def _proj_res(x, a_bf, w_bf, tm):
    m, d = x.shape
    return pl.pallas_call(
        _proj_res_body,
        grid=(m // tm,),
        in_specs=[_row_spec(tm, d), _row_spec(tm, d), _const_spec((d, d))],
        out_specs=_row_spec(tm, d),
        out_shape=jax.ShapeDtypeStruct((m, d), F32),
        compiler_params=_params(1),
        name="attn_out_proj",
    )(x, a_bf, w_bf)


def _shifted(h, prev, stride):
    tm = h.shape[0]
    if stride % 8 == 0:
        return jnp.concatenate([prev, h[:tm - stride]], axis=0)
    assert stride == 1
    row = lax.broadcasted_iota(jnp.int32, h.shape, 0)
    return jnp.where(row == 0, prev, pltpu.roll(h, 1, 0))


def _rwkv_proj_body(x_ref, g_ref, sh0_ref, mix_ref, wrkv_ref, w0_ref, d1_ref, d2_ref, a0_ref, i1_ref, i2_ref,
                    g1_ref, g2_ref, r_out, lw_out, k_out, v_out, a_out, g_out, sh_out,
                    carry_sc, *, stride, tiles_per_seq):
    i = pl.program_id(0)
    h = _rms(x_ref[...], g_ref[...])
    tm = h.shape[0]

    @pl.when(i % tiles_per_seq == 0)
    def _():
        carry_sc[...] = sh0_ref[...]

    hp = _shifted(h, carry_sc[...], stride)
    carry_sc[...] = h[tm - stride:]
    sh_out[...] = h[tm - stride:]
    xx = hp - h
    mix = mix_ref[...]

    def xm(s):
        return (h + xx * mix[s:s + 1]).astype(BF16)

    r_out[...] = _dot(xm(0), wrkv_ref[0])
    k_out[...] = _dot(xm(1), wrkv_ref[1])
    v_out[...] = _dot(xm(2), wrkv_ref[2])
    lora_w = _dot(jnp.tanh(_dot(xm(3), d1_ref[...])).astype(BF16), d2_ref[...])
    z = -(w0_ref[...] + lora_w)
    softplus = jnp.maximum(z, 0.0) + jnp.log(1.0 + jnp.exp(-jnp.abs(z)))
    lw_out[...] = -jnp.exp(-softplus - 0.5)
    a_out[...] = jax.nn.sigmoid(a0_ref[...] + _dot(_dot(xm(4), i1_ref[...]).astype(BF16), i2_ref[...]))
    g_out[...] = _dot(jax.nn.sigmoid(_dot(xm(5), g1_ref[...])).astype(BF16), g2_ref[...])


def _rwkv_proj(x, norm_g, shift0, w, tm, stride, tiles_per_seq):
    m, d = x.shape
    n_seq_blocks = shift0.shape[0]
    row = _row_spec(tm, d)
    f = jax.ShapeDtypeStruct((m, d), F32)
    vec = _const_spec((1, d))
    seq_spec = pl.BlockSpec((None, stride, d), lambda i: (i // tiles_per_seq, 0, 0))
    lora = w["d1"].shape[1]
    gate = w["g1"].shape[1]
    return pl.pallas_call(
        functools.partial(_rwkv_proj_body, stride=stride, tiles_per_seq=tiles_per_seq),
        grid=(m // tm,),
        in_specs=[row, vec, seq_spec, _const_spec((6, d)), _const_spec((3, d, d)),
                  vec, _const_spec((d, lora)), _const_spec((lora, d)),
                  vec, _const_spec((d, lora)), _const_spec((lora, d)),
                  _const_spec((d, gate)), _const_spec((gate, d))],
        out_specs=[row] * 6 + [seq_spec],
        out_shape=[f] * 6 + [jax.ShapeDtypeStruct((n_seq_blocks, stride, d), F32)],
        scratch_shapes=[pltpu.VMEM((stride, d), F32)],
        compiler_params=_params(1),
        name="rwkv_proj",
    )(x, norm_g.reshape(1, d), shift0, w["mix"], w["rkv"], w["w0"], w["d1"], w["d2"], w["a0"], w["i1"], w["i2"],
      w["g1"], w["g2"])


def _head_sums(x):
    rows, d = x.shape
    assert LANES == 2 * RWKV_HEAD and d % LANES == 0
    low = lax.broadcasted_iota(jnp.int32, (rows, LANES), 1) < RWKV_HEAD
    out = []
    for c in range(d // LANES):
        xc = x[:, c * LANES:(c + 1) * LANES]
        s_lo = jnp.sum(jnp.where(low, xc, 0.0), axis=-1, keepdims=True)
        s_hi = jnp.sum(jnp.where(low, 0.0, xc), axis=-1, keepdims=True)
        out.append(jnp.where(low, s_lo, s_hi))
    return jnp.concatenate(out, axis=1)


def _wkv_body(*refs, n_heads, chunk, project):
    chunk_refs, s0_ref = refs[:11], refs[11]
    if project:
        x_ref, wo_ref, o_ref, sout_ref, s_sc, y_sc, gated_sc = refs[12:]
    else:
        o_ref, sout_ref, s_sc, y_sc = refs[12:]
        gated_sc = o_ref
    c = pl.program_id(1)

    @pl.when(c == 0)
    def _():
        s_sc[...] = s0_ref[...]

    state = [s_sc[h] for h in range(n_heads)]
    for sub in range(y_sc.shape[0] // chunk):
        rows = slice(sub * chunk, (sub + 1) * chunk)
        state = _wkv_chunk(rows, *chunk_refs, gated_sc, y_sc, state)
    for h in range(n_heads):
        s_sc[h] = state[h]
    if project:
        o_ref[...] = x_ref[...] + _dot(gated_sc[...], wo_ref[...])

    @pl.when(c == pl.num_programs(1) - 1)
    def _():
        sout_ref[...] = s_sc[...]


def _wkv_chunk(rows, r_ref, lw_ref, k_ref, v_ref, ia_ref, g_ref, kk_ref, ka_ref, rk_ref, lg_ref, lb_ref,
               o_ref, y_sc, s_old):
    L = rows.stop - rows.start
    N = RWKV_HEAD
    n_heads = len(s_old)

    iclr = ia_ref[rows, :]
    k_raw = k_ref[rows, :]
    kk = k_raw * kk_ref[...]
    kk = kk / jnp.maximum(jnp.sqrt(_head_sums(kk * kk)), 1e-12)
    a_in = -kk
    b_in = kk * iclr
    k_in = k_raw * (1.0 + (iclr - 1.0) * ka_ref[...])
    r_in = r_ref[rows, :]
    v_in = v_ref[rows, :]
    bonus = _head_sums(r_in * k_in * rk_ref[...]) * v_in

    row = lax.broadcasted_iota(jnp.int32, (L, L), 0)
    col = lax.broadcasted_iota(jnp.int32, (L, L), 1)
    tri = (row >= col).astype(F32).astype(BF16)
    eye = (row == col).astype(F32)
    row2 = lax.broadcasted_iota(jnp.int32, (2 * L, 2 * L), 0)
    tok2 = lax.broadcasted_iota(jnp.int32, (2 * L, 2 * L), 1) % L
    gram_keep = tok2 < jnp.where(row2 < L, row2, row2 - L + 1)

    lw = lw_ref[rows, :]
    hi = lw.astype(BF16)
    r1 = lw - hi.astype(F32)
    mid = r1.astype(BF16)
    lo = (r1 - mid.astype(F32)).astype(BF16)
    cum = _dot(tri, hi) + _dot(tri, mid) + _dot(tri, lo)
    wl = cum[L - 1:L]
    e_in = jnp.exp(cum)
    e_out = jnp.exp(-cum)
    e_tail = jnp.exp(wl - cum)
    at = (a_in * jnp.exp(cum - lw)).astype(BF16)
    rt = (r_in * e_in).astype(BF16)
    bt = (b_in * e_out).astype(BF16)
    kt = (k_in * e_out).astype(BF16)
    bw = (b_in * e_tail).astype(BF16)
    kw = (k_in * e_tail).astype(BF16)
    vb = v_in.astype(BF16)
    ewl = jnp.exp(wl)

    heads = range(n_heads)
    hs = [slice(h * N, (h + 1) * N) for h in heads]
    ar = [jnp.concatenate([at[:, s], rt[:, s]], axis=0) for s in hs]
    bk_t = [jnp.concatenate([bt[:, s], kt[:, s]], axis=0) for s in hs]
    gram = [jnp.where(gram_keep, _dot_nt(ar[h], bk_t[h]), 0.0) for h in heads]
    p_s = [_dot_nt(ar[h], s_old[h].astype(BF16)) for h in heads]
    a_ab = [g[:L, :L] for g in gram]
    zero_v = jnp.zeros((L, N), BF16)
    rhs = [(p_s[h][:L] + _dot(gram[h][:L].astype(BF16), jnp.concatenate([zero_v, vb[:, hs[h]]], axis=0))
            ).astype(BF16) for h in heads]

    t_inv = [eye + a for a in a_ab]
    a_pow = []
    for a in a_ab:
        ab = a.astype(BF16)
        a_pow.append(_dot(ab, ab))
    n_steps = int(math.log2(L)) - 1
    for step in range(n_steps):
        last = step == n_steps - 1
        for h in heads:
            pb = a_pow[h].astype(BF16)
            if last:
                t_inv[h] = t_inv[h] + _dot(t_inv[h].astype(BF16), pb)
            else:
                z = _dot(jnp.concatenate([t_inv[h], a_pow[h]], axis=0).astype(BF16), pb)
                t_inv[h] = t_inv[h] + z[:L]
                a_pow[h] = z[L:]

    u = [_dot(t_inv[h].astype(BF16), rhs[h]).astype(BF16) for h in heads]
    uv_l = [jnp.concatenate([u[h], vb[:, hs[h]]], axis=0) for h in heads]
    for h in heads:
        y_sc[rows, hs[h]] = p_s[h][L:] + _dot(gram[h][L:].astype(BF16), uv_l[h])
    s_new = []
    for h in heads:
        uv = uv_l[h]
        bk = jnp.concatenate([bw[:, hs[h]], kw[:, hs[h]]], axis=0)
        s_new.append(s_old[h] * ewl[:, hs[h]] + _dot_tn(uv, bk))

    y = y_sc[rows, :]
    inv_n = 1.0 / N
    yc = y - _head_sums(y) * inv_n
    var = _head_sums(yc * yc) * inv_n
    yn = yc * lax.rsqrt(var + LNX_EPS) * lg_ref[...] + lb_ref[...]
    o_ref[rows, :] = ((yn + bonus) * g_ref[rows, :]).astype(BF16)
    return s_new


def _wkv(r, lw, k, v, iclr, g, w, s0, chunk, chunks_per_step, x=None):
    m, d = r.shape
    n_seq, n_heads = s0.shape[0], s0.shape[1]
    step_rows = chunk * chunks_per_step
    n_chunks = m // (n_seq * step_rows)
    row = pl.BlockSpec((step_rows, d), lambda s, c: (s * n_chunks + c, 0))
    st = pl.BlockSpec((None, n_heads, RWKV_HEAD, RWKV_HEAD), lambda s, c: (s, 0, 0, 0))
    vec = _const_spec((1, d))
    project = x is not None
    ins = [r, lw, k, v, iclr, g, w["k_k"], w["k_a"], w["r_k"], w["lnx_g"], w["lnx_b"], s0]
    in_specs = [row] * 6 + [vec] * 5 + [st]
    scratch = [pltpu.VMEM((n_heads, RWKV_HEAD, RWKV_HEAD), F32), pltpu.VMEM((step_rows, d), F32)]
    if project:
        ins += [x, w["wo"]]
        in_specs += [row, _const_spec((d, d))]
        scratch.append(pltpu.VMEM((step_rows, d), BF16))
    return pl.pallas_call(
        functools.partial(_wkv_body, n_heads=n_heads, chunk=chunk, project=project),
        grid=(n_seq, n_chunks),
        in_specs=in_specs,
        out_specs=[row, st],
        out_shape=[jax.ShapeDtypeStruct((m, d), F32 if project else BF16), jax.ShapeDtypeStruct(s0.shape, F32)],
        scratch_shapes=scratch,
        compiler_params=_params(2),
        name="wkv7_chunked",
    )(*ins)


def _ffn_body(*refs, stride, tiles_per_seq, n_chunks, has_buf, final):
    x_ref, g_ref, wg_ref, wu_ref, cw_ref, cb_ref, wd_ref = refs[:7]
    pos = 7
    buf_ref = gf_ref = None
    if has_buf:
        buf_ref = refs[pos]
        pos += 1
    if final:
        gf_ref = refs[pos]
        pos += 1
    o_ref, cv_ref, carry_sc, acc_sc = refs[pos:pos + 4]
    i = pl.program_id(0)
    x = x_ref[...]
    tm = x.shape[0]
    fc = wg_ref.shape[1] // n_chunks
    h = _rms(x, g_ref[...]).astype(BF16)
    acc_sc[...] = jnp.zeros(acc_sc.shape, F32)
    if not has_buf:
        @pl.when(i % tiles_per_seq == 0)
        def _():
            carry_sc[...] = jnp.zeros(carry_sc.shape, F32)
        row8 = lax.broadcasted_iota(jnp.int32, (8, fc), 0)

    for c in range(n_chunks):
        cs = slice(c * fc, (c + 1) * fc)
        u = _dot(h, wg_ref[:, cs])
        up = _dot(h, wu_ref[:, cs])
        cw = cw_ref[:, cs]
        if has_buf:
            full = jnp.concatenate([buf_ref[:, cs], u], axis=0)
            u2 = full[0:tm]
            u1 = full[stride:stride + tm]
            cv_ref[:, cs] = full[tm:tm + 2 * stride]
        else:
            prev = carry_sc[:, cs]
            r1 = pltpu.roll(u, 1, 0)
            r2 = pltpu.roll(u, 2, 0)
            u1 = jnp.concatenate([jnp.where(row8 == 0, prev[1:2], r1[:8]), r1[8:]], axis=0)
            u2 = jnp.concatenate(
                [jnp.where(row8 == 0, prev[0:1], jnp.where(row8 == 1, prev[1:2], r2[:8])), r2[8:]], axis=0)
            carry_sc[:, cs] = u[tm - 2:tm]
            cv_ref[:, cs] = u[tm - 2:tm]
        cv = cb_ref[:, cs] + u2 * cw[0:1] + u1 * cw[1:2] + u * cw[2:3]
        half = 0.5 * cv
        act = (half * (1.0 + jnp.tanh(half)) * up).astype(BF16)
        acc_sc[...] += _dot(act, wd_ref[cs, :])
    out = x + acc_sc[...]
    if final:
        out = _rms(out, gf_ref[...])
    o_ref[...] = out


def _ffn(x, norm_g, w, tm, stride, tiles_per_seq, buf0, final_g):
    m, d = x.shape
    f = w["wg"].shape[1]
    n_chunks = f // FFN_CHUNK
    has_buf = buf0 is not None
    final = final_g is not None
    n_seq_blocks = (m // tm) // tiles_per_seq
    keep = 2 * stride
    seq_spec = pl.BlockSpec((None, keep, f), lambda i: (i // tiles_per_seq, 0, 0))
    ins = [x, norm_g.reshape(1, d), w["wg"], w["wu"], w["cw"], w["cb"], w["wd"]]
    in_specs = [_row_spec(tm, d), _const_spec((1, d)), _const_spec((d, f)), _const_spec((d, f)),
                _const_spec((3, f)), _const_spec((1, f)), _const_spec((f, d))]
    if has_buf:
        ins.append(buf0)
        in_specs.append(seq_spec)
    if final:
        ins.append(final_g.reshape(1, d))
        in_specs.append(_const_spec((1, d)))
    return pl.pallas_call(
        functools.partial(_ffn_body, stride=stride, tiles_per_seq=tiles_per_seq, n_chunks=n_chunks,
                          has_buf=has_buf, final=final),
        grid=(m // tm,),
        in_specs=in_specs,
        out_specs=[_row_spec(tm, d), seq_spec],
        out_shape=[jax.ShapeDtypeStruct((m, d), F32),
                   jax.ShapeDtypeStruct((n_seq_blocks, keep, f), F32)],
        scratch_shapes=[pltpu.VMEM((2, f), F32), pltpu.VMEM((tm, d), F32)],
        compiler_params=_params(1),
        name="conv_ffn",
    )(*ins)


def _pad_cols(w, n):
    return jnp.pad(w, ((0, 0), (0, n - w.shape[1])))


def _pad_rows(w, n):
    return jnp.pad(w, ((0, n - w.shape[0]), (0, 0)))


def _round_up(n, mult):
    return -(-n // mult) * mult


def _row_tile(rows):
    tm = min(ROW_TILE, rows)
    assert rows % tm == 0
    return tm


def kernel(x_prompt, x_sample, cache_k, cache_v, page_table, state_shift, state_wkv, state_ffn_conv, attn_norm_g, w_qkv, lambda_q1, lambda_k1, lambda_q2, lambda_k2, subln_g, w_o_attn, rwkv_norm_g, rwkv_mix, w_rkv, w_decay0, w_decay1, w_decay2, w_iclr0, w_iclr1, w_iclr2, rwkv_g1, rwkv_g2, k_k, k_a, r_k, lnx_g, lnx_b, w_o_rwkv, ffn_norm_g, ffn_w_gate, ffn_w_up, ffn_conv_w, ffn_conv_b, ffn_w_down, final_norm_g):
    B, T, D = x_prompt.shape
    DB, TS, _ = x_sample.shape
    depth = ffn_norm_g.shape[0]
    F = ffn_w_gate.shape[2]
    n_attn_heads = D // (2 * ATTN_HEAD_DIM)
    n_rwkv_heads = D // RWKV_HEAD
    hd = 2 * ATTN_HEAD_DIM
    tm_p = _row_tile(T)
    tps_p = T // tm_p
    ms = TS * DB
    assert F % FFN_CHUNK == 0 and DB % 8 == 0
    wkv_cps = math.gcd(WKV_CHUNKS_PER_STEP, T // WKV_CHUNK)
    assert T % (WKV_CHUNK * wkv_cps) == 0

    xp = x_prompt.reshape(B * T, D)
    xs = x_sample.transpose(1, 0, 2).reshape(ms, D)

    kp_l, vp_l, ks_l, vs_l = [], [], [], []
    shp_l, shs_l, wkp_l, wks_l = [], [], [], []
    cvp_l, cvs_l = [], []
    for i in range(depth):
        j = i // 2
        if i % 2 == 0:
            lam_init = 0.8 - 0.6 * math.exp(-0.3 * i)
            lams = [v[j].reshape(1, -1) for v in (lambda_q1, lambda_k1, lambda_q2, lambda_k2)]
            wqkv = w_qkv[j].astype(BF16)
            wo = w_o_attn[j].astype(BF16)
            q0b, q1b, k32, v32, kb, vb = _qkv(xp, attn_norm_g[j], wqkv, tm_p, for_flash=True)
            xp = _flash(xp, q0b, q1b, kb, vb, lams, subln_g[j], wo, B, T, lam_init)
            kp_l.append(k32.reshape(B, T, n_attn_heads, hd))
            vp_l.append(v32.reshape(B, T, n_attn_heads, hd))
            q0b, q1b, k32, v32 = _qkv(xs, attn_norm_g[j], wqkv, ms)
            qb = q0b + q1b
            k_new = k32.reshape(TS, DB, n_attn_heads, hd).transpose(1, 0, 2, 3)
            v_new = v32.reshape(TS, DB, n_attn_heads, hd).transpose(1, 0, 2, 3)
            ks_l.append(k_new)
            vs_l.append(v_new)
            q5 = qb.reshape(TS, DB, n_attn_heads, 2, ATTN_HEAD_DIM).transpose(1, 2, 3, 0, 4)
            q_all = (q5[:, :, :, :, None, :] * jnp.eye(2, dtype=BF16)[None, None, :, None, :, None])
            q_all = q_all.reshape(DB, n_attn_heads * 2 * TS, hd)
            new_pad = _round_up(TS, 16)
            pad = ((0, 0), (0, new_pad - TS), (0, 0), (0, 0))
            o = _decode(q_all, cache_k, cache_v, j, page_table, jnp.pad(k_new, pad), jnp.pad(v_new, pad),
                        lams, subln_g[j], lam_init)
            o = o.reshape(DB, n_attn_heads, 2, TS, hd)[:, :, 0].transpose(2, 0, 1, 3).reshape(ms, D)
            xs = _proj_res(xs, o.astype(BF16), wo, ms)
        else:
            lora = _round_up(w_decay1.shape[2], LANES)
            gate = _round_up(rwkv_g1.shape[2], LANES)
            w = dict(
                mix=rwkv_mix[j], rkv=w_rkv[j].astype(BF16),
                w0=w_decay0[j].reshape(1, D), d1=_pad_cols(w_decay1[j], lora).astype(BF16),
                d2=_pad_rows(w_decay2[j], lora).astype(BF16),
                a0=w_iclr0[j].reshape(1, D), i1=_pad_cols(w_iclr1[j], lora).astype(BF16),
                i2=_pad_rows(w_iclr2[j], lora).astype(BF16),
                g1=_pad_cols(rwkv_g1[j], gate).astype(BF16), g2=_pad_rows(rwkv_g2[j], gate).astype(BF16),
                k_k=k_k[j].reshape(1, D), k_a=k_a[j].reshape(1, D), r_k=r_k[j].reshape(1, D),
                lnx_g=lnx_g[j].reshape(1, D), lnx_b=lnx_b[j].reshape(1, D),
                wo=w_o_rwkv[j].astype(BF16))
            *rwkv_in, sh = _rwkv_proj(xp, rwkv_norm_g[j], jnp.zeros((B, 1, D), F32), w, tm_p, 1, tps_p)
            xp, s_fin = _wkv(*rwkv_in, w, jnp.zeros((B, n_rwkv_heads, RWKV_HEAD, RWKV_HEAD), F32), WKV_CHUNK,
                             wkv_cps, x=xp)
            shp_l.append(sh.reshape(B, D))
            wkp_l.append(s_fin)
            *rwkv_in, sh = _rwkv_proj(xs, rwkv_norm_g[j], state_shift[j][None], w, ms, DB, 1)

            chunk_s = max(16, pl.next_power_of_2(TS))

            def to_seq(a):
                a = a.reshape(TS, DB, D).transpose(1, 0, 2)
                return jnp.pad(a, ((0, 0), (0, chunk_s - TS), (0, 0))).reshape(DB * chunk_s, D)

            y, s_fin = _wkv(*(to_seq(a) for a in rwkv_in), w, state_wkv[j], chunk_s, 1)
            y = y.reshape(DB, chunk_s, D)[:, :TS].transpose(1, 0, 2).reshape(ms, D)
            xs = _proj_res(xs, y, w["wo"], ms)
            shs_l.append(sh.reshape(DB, D))
            wks_l.append(s_fin)
        w = dict(
            wg=ffn_w_gate[i].astype(BF16), wu=ffn_w_up[i].astype(BF16), cw=ffn_conv_w[i],
            cb=ffn_conv_b[i].reshape(1, F), wd=ffn_w_down[i].astype(BF16))
        final_g = final_norm_g if i == depth - 1 else None
        tm_f = min(FFN_ROW_TILE, T)
        xp, cv = _ffn(xp, ffn_norm_g[i], w, tm_f, 1, T // tm_f, None, final_g)
        cvp_l.append(cv)
        buf0 = state_ffn_conv[i].transpose(1, 0, 2).reshape(1, 2 * DB, F)
        xs, cv = _ffn(xs, ffn_norm_g[i], w, ms, DB, 1, buf0, final_g)
        cvs_l.append(cv.reshape(2, DB, F).transpose(1, 0, 2))

    y_prompt = xp.reshape(B, T, D)
    y_sample = xs.reshape(TS, DB, D).transpose(1, 0, 2)
    return (y_prompt, y_sample,
            jnp.stack(kp_l), jnp.stack(vp_l), jnp.stack(ks_l), jnp.stack(vs_l),
            jnp.stack(shp_l), jnp.stack(shs_l), jnp.stack(wkp_l), jnp.stack(wks_l),
            jnp.stack(cvp_l), jnp.stack(cvs_l))
```
